```python
import jax, jax.numpy as jnp
from jax import lax
import numpy as np

D_MODEL = 1024
BATCH = 4
SEQ = 4096
DEPTH = 2
DEC_BATCH = 32
DEC_SEQ = 1
PAST_LEN = 8192
PAGE_SIZE = 128

N_BRANCH = 3
MIX_W = D_MODEL // 2
GM_CHUNK = 128
GM_GROUPS = 4
GM_WIDTH = MIX_W
GM_GW = GM_WIDTH // GM_GROUPS
SB_HEAD_DIM = 64
SB_HEADS = MIX_W // SB_HEAD_DIM
SB_WIDTH = SB_HEADS * SB_HEAD_DIM
SB_QBLOCK = 128
SB_BIAS_INIT = -8.0
GLA_HEADS = 4
GLA_VW = MIX_W
GLA_DV = GLA_VW // GLA_HEADS
GLA_DK = GLA_DV // 2
GLA_KW = GLA_HEADS * GLA_DK
GLA_RANK = 16
GLA_TAU = 16.0
GLA_CHUNK = 64
D_FF = ((8 * D_MODEL // 3 + 127) // 128) * 128
CONV_W = 3
PLE_DIM = 256
LN_EPS = 1e-5
DN_ALPHA = (2.0 * DEPTH) ** 0.25
DN_BETA = (8.0 * DEPTH) ** -0.25
IN_SIZES = (GM_WIDTH, GM_WIDTH, SB_WIDTH, SB_WIDTH, SB_WIDTH, GLA_KW, GLA_KW, GLA_VW, GLA_VW, GLA_RANK, N_BRANCH * D_MODEL)
D_IN = sum(IN_SIZES)
IN_SPLITS = tuple(int(c) for c in np.cumsum(IN_SIZES)[:-1])

kernel_name = "hybrid_gmlp_stickbreak_gla_decode_step"


def layer_norm(x, g, b):
    xf = x.astype(jnp.float32)
    mu = jnp.mean(xf, axis=-1, keepdims=True)
    var = jnp.mean(jnp.square(xf - mu), axis=-1, keepdims=True)
    return ((xf - mu) * lax.rsqrt(var + LN_EPS)).astype(x.dtype) * g + b


def chunk_gmlp(u, v, w_s, b_s, ln_g, ln_b):
    bsz, s, _ = u.shape
    n = -(-s // GM_CHUNK)
    v = layer_norm(v, ln_g, ln_b)
    vp = jnp.pad(v, ((0, 0), (0, n * GM_CHUNK - s), (0, 0))).reshape(bsz, n, GM_CHUNK, GM_GROUPS, GM_GW)
    causal = jnp.tril(jnp.ones((GM_CHUNK, GM_CHUNK), dtype=bool))
    ws = jnp.where(causal[None], w_s, 0.0)
    sv = jnp.einsum('gts,bnsgc->bntgc', ws, vp) + b_s.T[None, None, :, :, None]
    sv = sv.reshape(bsz, n * GM_CHUNK, GM_WIDTH)[:, :s]
    return u * sv, v


def stick_breaking(q, k, v, bias, q_pos, k_pos):
    bsz, sq, h, d = q.shape
    blk = SB_QBLOCK if sq % SB_QBLOCK == 0 else sq
    nb = sq // blk
    qb = q.reshape(bsz, nb, blk, h, d).transpose(1, 0, 2, 3, 4)
    pb = q_pos.reshape(nb, blk)
    scale = d ** -0.5
    bias_f = bias.astype(jnp.float32)[None, :, None, None]

    def one_block(args):
        qi, pi = args
        z = jnp.einsum('bthd,bshd->bhts', qi, k).astype(jnp.float32) * scale + bias_f
        causal = k_pos[None, :] < pi[:, None]
        log_keep = jnp.where(causal, jax.nn.log_sigmoid(-z), 0.0)
        shifted = jnp.concatenate([log_keep[..., 1:], jnp.zeros_like(log_keep[..., :1])], axis=-1)
        remain = lax.cumsum(shifted, axis=3, reverse=True)
        w = jnp.where(causal, jnp.exp(jax.nn.log_sigmoid(z) + remain), 0.0)
        return jnp.einsum('bhts,bshd->bthd', w.astype(v.dtype), v)

    out = lax.map(one_block, (qb, pb))
    return out.transpose(1, 0, 2, 3, 4).reshape(bsz, sq, h, d)


def gla_chunked(q, k, v, log_a, s0):
    bsz, s, h, dk = q.shape
    dv = v.shape[-1]
    c = GLA_CHUNK if s % GLA_CHUNK == 0 else s
    n = s // c

    def to_chunks(t):
        return t.reshape(bsz, n, c, h, t.shape[-1]).transpose(1, 0, 3, 2, 4)

    causal = jnp.tril(jnp.ones((c, c), dtype=bool))[:, :, None]

    def step(state, inp):
        qc, kc, vc, gc = inp
        b = jnp.cumsum(gc.astype(jnp.float32), axis=2)
        o_inter = jnp.einsum('bhtk,bhkv->bhtv', qc * jnp.exp(b), state)
        diff = b[:, :, :, None, :] - b[:, :, None, :, :]
        decay = jnp.where(causal, jnp.exp(jnp.where(causal, diff, 0.0)), 0.0)
        att = jnp.einsum('bhtk,bhsk,bhtsk->bhts', qc, kc, decay)
        o = o_inter + jnp.einsum('bhts,bhsv->bhtv', att, vc)
        b_last = b[:, :, -1:, :]
        new_state = jnp.exp(b_last[:, :, 0, :, None]) * state + jnp.einsum('bhsk,bhsv->bhkv', kc * jnp.exp(b_last - b), vc)
        return new_state.astype(state.dtype), o

    s_final, o = lax.scan(step, s0, (to_chunks(q), to_chunks(k), to_chunks(v), to_chunks(log_a)))
    o = o.transpose(1, 0, 3, 2, 4).reshape(bsz, s, h, dv).astype(v.dtype)
    return o, s_final


def conv_ffn(x, conv_prev, w_up, conv_w, conv_b, w_down):
    s = x.shape[1]
    hup = x @ w_up
    hp = jnp.concatenate([conv_prev, hup], axis=1)
    hc = conv_b + conv_w[0] * hp[:, 0:s]
    for i in range(1, CONV_W):
        hc = hc + conv_w[i] * hp[:, i:i + s]
    a, g = jnp.split(hc, 2, axis=-1)
    return (jax.nn.gelu(a) * g) @ w_down, hp[:, -(CONV_W - 1):]


def trunk_layer(x, p_i, past_kv, gla_s0, conv_prev, q_pos, k_pos, lw):
    (w_in, b_in, gm_ws, gm_bs, gm_ln_g, gm_ln_b, sb_bias, gla_wa2, gla_ba, gla_gn_g, gla_gn_b,
     w_br_a, w_br_b, w_br_c, w_out, ln1_g, ln1_b, w_up, conv_w, conv_b, w_down,
     w_ple, w_pgate, ln2_g, ln2_b) = lw
    bsz, s, _ = x.shape
    proj = x @ w_in + b_in
    (gm_u, gm_v, sb_q, sb_k, sb_v, gl_q, gl_k, gl_v, gl_r, gl_a, gates) = jnp.split(proj, IN_SPLITS, axis=-1)

    y_a, gm_state = chunk_gmlp(jax.nn.gelu(gm_u), jax.nn.gelu(gm_v), gm_ws, gm_bs, gm_ln_g, gm_ln_b)

    q_b = sb_q.reshape(bsz, s, SB_HEADS, SB_HEAD_DIM)
    k_b = sb_k.reshape(bsz, s, SB_HEADS, SB_HEAD_DIM)
    v_b = sb_v.reshape(bsz, s, SB_HEADS, SB_HEAD_DIM)
    if past_kv is None:
        k_all, v_all = k_b, v_b
    else:
        k_all = jnp.concatenate([past_kv[0], k_b], axis=1)
        v_all = jnp.concatenate([past_kv[1], v_b], axis=1)
    y_b = stick_breaking(q_b, k_all, v_all, sb_bias, q_pos, k_pos).reshape(bsz, s, SB_WIDTH)

    q_c = (gl_q * GLA_DK ** -0.5).reshape(bsz, s, GLA_HEADS, GLA_DK)
    k_c = gl_k.reshape(bsz, s, GLA_HEADS, GLA_DK)
    v_c = gl_v.reshape(bsz, s, GLA_HEADS, GLA_DV)
    log_a = (jax.nn.log_sigmoid((gl_a @ gla_wa2 + gla_ba).astype(jnp.float32)) / GLA_TAU).reshape(bsz, s, GLA_HEADS, GLA_DK)
    o_c, gla_new = gla_chunked(q_c, k_c, v_c, log_a, gla_s0)
    o_c = layer_norm(o_c, gla_gn_g.reshape(GLA_HEADS, GLA_DV), gla_gn_b.reshape(GLA_HEADS, GLA_DV)).reshape(bsz, s, GLA_VW)
    y_c = jax.nn.silu(gl_r) * o_c

    gate = jax.nn.sigmoid(gates).reshape(bsz, s, N_BRANCH, D_MODEL)
    merged = gate[:, :, 0] * (y_a @ w_br_a) + gate[:, :, 1] * (y_b @ w_br_b) + gate[:, :, 2] * (y_c @ w_br_c)
    x1 = layer_norm(DN_ALPHA * x + merged @ w_out, ln1_g, ln1_b)

    f, conv_new = conv_ffn(x1, conv_prev, w_up, conv_w, conv_b, w_down)
    e = (p_i @ w_ple) * jax.nn.sigmoid(x1 @ w_pgate)
    x2 = layer_norm(DN_ALPHA * x1 + f + e, ln2_g, ln2_b)
    return x2, gm_state, k_b, v_b, gla_new, conv_new


def setup_inputs(seed: int = 0) -> dict:
    key = jax.random.key(seed)
    ks = iter(jax.random.split(key, 48))

    def nrm(shape, scale):
        return jax.random.normal(next(ks), shape, jnp.float32) * scale

    n_pages = PAST_LEN // PAGE_SIZE
    n_used = DEC_BATCH * n_pages
    n_pool = (n_used * 5) // 4
    d2 = 2 * D_FF
    page_table = jax.random.permutation(next(ks), n_pool)[:n_used].reshape(DEC_BATCH, n_pages).astype(jnp.int32)
    return {
        'x_prompt': nrm((BATCH, SEQ, D_MODEL), 1.0),
        'x_sample': nrm((DEC_BATCH, DEC_SEQ, D_MODEL), 1.0),
        'cache_k': nrm((DEPTH, n_pool, PAGE_SIZE, SB_HEADS, SB_HEAD_DIM), 1.0),
        'cache_v': nrm((DEPTH, n_pool, PAGE_SIZE, SB_HEADS, SB_HEAD_DIM), 1.0),
        'state_gla': nrm((DEPTH, DEC_BATCH, GLA_HEADS, GLA_DK, GLA_DV), 0.3),
        'state_conv': nrm((DEPTH, DEC_BATCH, CONV_W - 1, d2), 1.0),
        'page_table': page_table,
        'p_prompt': nrm((DEPTH, BATCH, SEQ, PLE_DIM), 1.0),
        'p_sample': nrm((DEPTH, DEC_BATCH, DEC_SEQ, PLE_DIM), 1.0),
        'w_in': nrm((DEPTH, D_MODEL, D_IN), D_MODEL ** -0.5),
        'b_in': nrm((DEPTH, D_IN), 0.02),
        'gm_ws': nrm((DEPTH, GM_GROUPS, GM_CHUNK, GM_CHUNK), GM_CHUNK ** -0.5),
        'gm_bs': 1.0 + nrm((DEPTH, GM_GROUPS, GM_CHUNK), 0.02),
        'gm_ln_g': 1.0 + nrm((DEPTH, GM_WIDTH), 0.02),
        'gm_ln_b': nrm((DEPTH, GM_WIDTH), 0.02),
        'sb_bias': SB_BIAS_INIT + nrm((DEPTH, SB_HEADS), 0.1),
        'gla_wa2': nrm((DEPTH, GLA_RANK, GLA_KW), GLA_RANK ** -0.5),
        'gla_ba': nrm((DEPTH, GLA_KW), 0.1),
        'gla_gn_g': 1.0 + nrm((DEPTH, GLA_VW), 0.02),
        'gla_gn_b': nrm((DEPTH, GLA_VW), 0.02),
        'w_br_a': nrm((DEPTH, GM_WIDTH, D_MODEL), GM_WIDTH ** -0.5 * DN_BETA),
        'w_br_b': nrm((DEPTH, SB_WIDTH, D_MODEL), SB_WIDTH ** -0.5 * DN_BETA),
        'w_br_c': nrm((DEPTH, GLA_VW, D_MODEL), GLA_VW ** -0.5 * DN_BETA),
        'w_out': nrm((DEPTH, D_MODEL, D_MODEL), D_MODEL ** -0.5 * DN_BETA),
        'ln1_g': 1.0 + nrm((DEPTH, D_MODEL), 0.02),
        'ln1_b': nrm((DEPTH, D_MODEL), 0.02),
        'w_up': nrm((DEPTH, D_MODEL, d2), D_MODEL ** -0.5),
        'conv_w': nrm((DEPTH, CONV_W, d2), CONV_W ** -0.5),
        'conv_b': nrm((DEPTH, d2), 0.02),
        'w_down': nrm((DEPTH, D_FF, D_MODEL), D_FF ** -0.5 * DN_BETA),
        'w_ple': nrm((DEPTH, PLE_DIM, D_MODEL), PLE_DIM ** -0.5 * DN_BETA),
        'w_pgate': nrm((DEPTH, D_MODEL, D_MODEL), D_MODEL ** -0.5),
        'ln2_g': 1.0 + nrm((DEPTH, D_MODEL), 0.02),
        'ln2_b': nrm((DEPTH, D_MODEL), 0.02),
    }


def reference(x_prompt, x_sample, cache_k, cache_v, state_gla, state_conv, page_table, p_prompt, p_sample,
              w_in, b_in, gm_ws, gm_bs, gm_ln_g, gm_ln_b, sb_bias, gla_wa2, gla_ba, gla_gn_g, gla_gn_b,
              w_br_a, w_br_b, w_br_c, w_out, ln1_g, ln1_b, w_up, conv_w, conv_b, w_down,
              w_ple, w_pgate, ln2_g, ln2_b):
    bp, sp = x_prompt.shape[0], x_prompt.shape[1]
    bd, sd = x_sample.shape[0], x_sample.shape[1]
    n_past = page_table.shape[1] * cache_k.shape[2]
    pos_p = jnp.arange(sp, dtype=jnp.int32)
    pos_d = n_past + jnp.arange(sd, dtype=jnp.int32)
    kpos_d = jnp.arange(n_past + sd, dtype=jnp.int32)

    yp, yd = x_prompt, x_sample
    kp_l, vp_l, kd_l, vd_l, gp_l, gd_l, cp_l, cd_l, gmd_l = [], [], [], [], [], [], [], [], []
    for i in range(DEPTH):
        lw = (w_in[i], b_in[i], gm_ws[i], gm_bs[i], gm_ln_g[i], gm_ln_b[i], sb_bias[i], gla_wa2[i], gla_ba[i],
              gla_gn_g[i], gla_gn_b[i], w_br_a[i], w_br_b[i], w_br_c[i], w_out[i], ln1_g[i], ln1_b[i],
              w_up[i], conv_w[i], conv_b[i], w_down[i], w_ple[i], w_pgate[i], ln2_g[i], ln2_b[i])
        gla0 = jnp.zeros((bp, GLA_HEADS, GLA_DK, GLA_DV), x_prompt.dtype)
        conv0 = jnp.zeros((bp, CONV_W - 1, 2 * D_FF), x_prompt.dtype)
        yp, _, k_p, v_p, g_p, c_p = trunk_layer(yp, p_prompt[i], None, gla0, conv0, pos_p, pos_p, lw)
        past_k = cache_k[i][page_table].reshape(bd, n_past, SB_HEADS, SB_HEAD_DIM)
        past_v = cache_v[i][page_table].reshape(bd, n_past, SB_HEADS, SB_HEAD_DIM)
        yd, gm_d, k_d, v_d, g_d, c_d = trunk_layer(yd, p_sample[i], (past_k, past_v), state_gla[i], state_conv[i],
                                                   pos_d, kpos_d, lw)
        kp_l.append(k_p); vp_l.append(v_p); kd_l.append(k_d); vd_l.append(v_d)
        gp_l.append(g_p); gd_l.append(g_d); cp_l.append(c_p); cd_l.append(c_d); gmd_l.append(gm_d)

    return (yp, yd,
            jnp.stack(kp_l), jnp.stack(vp_l), jnp.stack(kd_l), jnp.stack(vd_l),
            jnp.stack(gp_l), jnp.stack(gd_l), jnp.stack(cp_l), jnp.stack(cd_l), jnp.stack(gmd_l))
```

```python
import functools

import numpy as np
import jax
import jax.numpy as jnp
from jax import lax
from jax.experimental import pallas as pl
from jax.experimental.pallas import tpu as pltpu

F32 = jnp.float32
BF16 = jnp.bfloat16

D_MODEL = 1024
DEPTH = 2
PAGE = 128
MIX_W = 512
GM_CHUNK = 128
GM_GROUPS = 4
SB_HEADS = 8
SB_HD = 64
GLA_HEADS = 4
GLA_DK = 64
GLA_DV = 128
GLA_KW = 256
GLA_RANK = 16
GLA_TAU = 16.0
GLA_CHUNK = 64
D_FF = 2816
D2 = 2 * D_FF
CONV_W = 3
PLE_DIM = 256
LN_EPS = 1e-5
DN_ALPHA = (2.0 * DEPTH) ** 0.25
LANES = 128
SUBLANES = 8

C_U, C_V, C_Q, C_K, C_VV = 0, 512, 1024, 1536, 2048
C_GQ, C_GK, C_GV, C_GR, C_GA, C_GATE = 2560, 2816, 3072, 3584, 4096, 4224
D_INP = C_GATE + 3 * D_MODEL

TM1 = 256
TM3 = 256
TQ = 256
TK = 256
FF_CH = 256
PAGES_PER_STEP = 8
VMEM_LIMIT = 56 * 1024 * 1024


def _nt(a, b):
    return lax.dot_general(a, b, (((1,), (1,)), ((), ())), preferred_element_type=F32)


def _tn(a, b):
    return lax.dot_general(a, b, (((0,), (0,)), ((), ())), preferred_element_type=F32)


def _nn(a, b):
    return jnp.dot(a, b, preferred_element_type=F32)


def _ln(x, g, b):
    mu = jnp.mean(x, axis=-1, keepdims=True)
    xc = x - mu
    var = jnp.mean(xc * xc, axis=-1, keepdims=True)
    return xc * lax.rsqrt(var + LN_EPS) * g + b


def _softplus(z):
    return jnp.maximum(z, 0.0) + jnp.log1p(jnp.exp(-jnp.abs(z)))


def _log_sigmoid(z):
    return jnp.minimum(z, 0.0) - jnp.log1p(jnp.exp(-jnp.abs(z)))


def _split_bf16(x):
    hi = x.astype(BF16)
    lo = (x - hi.astype(F32)).astype(BF16)
    return hi, lo


def _const_spec(shape):
    nd = len(shape)
    return pl.BlockSpec(shape, lambda *_: (0,) * nd, pipeline_mode=pl.Buffered(1))


def _p1_kernel(x_ref, win_ref, bin_ref, ws_ref, bsf_ref, gmg_ref, gmb_ref, wa2_ref, ba_ref, gng_ref, gnb_ref,
               wbra_ref, wbrc_ref, tri_ref,
               kf_ref, vf_ref, qb_ref, kb_ref, vb_ref, mac_ref, gb_ref, st_ref, s2_scr):
    tm = x_ref.shape[1]

    @pl.when(pl.program_id(1) == 0)
    def _():
        s2_scr[...] = jnp.zeros_like(s2_scr)

    xb = x_ref[0].astype(BF16)

    def proj(c0, c1):
        return _nn(xb, win_ref[:, c0:c1]) + bin_ref[:, c0:c1]

    q = proj(C_Q, C_K)
    k = proj(C_K, C_VV)
    v = proj(C_VV, C_GQ)
    kf_ref[0] = k
    vf_ref[0] = v
    qb_ref[0] = (q * (SB_HD ** -0.5)).astype(BF16)
    kb_ref[0] = k.astype(BF16)
    vb_ref[0] = v.astype(BF16)

    u = jax.nn.gelu(proj(C_U, C_V))
    vn = _ln(jax.nn.gelu(proj(C_V, C_Q)), gmg_ref[...], gmb_ref[...])
    vnb = vn.astype(BF16)
    r_i = lax.broadcasted_iota(jnp.int32, (GM_CHUNK, GM_CHUNK), 0)
    c_i = lax.broadcasted_iota(jnp.int32, (GM_CHUNK, GM_CHUNK), 1)
    ws_m = [jnp.where(r_i >= c_i, ws_ref[g], 0.0).astype(BF16) for g in range(GM_GROUPS)]
    sv_rows = []
    for n in range(tm // GM_CHUNK):
        cols = [_nn(ws_m[g], vnb[n * GM_CHUNK:(n + 1) * GM_CHUNK, g * LANES:(g + 1) * LANES])
                for g in range(GM_GROUPS)]
        sv_rows.append(jnp.concatenate(cols, axis=1) + bsf_ref[...])
    y_a = u * jnp.concatenate(sv_rows, axis=0)

    gq = proj(C_GQ, C_GK) * (GLA_DK ** -0.5)
    gk = proj(C_GK, C_GV)
    gv = proj(C_GV, C_GR).astype(BF16)
    ga = proj(C_GA, C_GATE).astype(BF16)
    la = _log_sigmoid(_nn(ga, wa2_ref[...]) + ba_ref[...]) * (1.0 / GLA_TAU)
    la_hi, la_lo = _split_bf16(la)
    tri = tri_ref[...]
    bcum = _nn(tri, la_hi) + _nn(tri, la_lo)
    head_of_lane = lax.broadcasted_iota(jnp.int32, (GLA_CHUNK, GLA_KW), 1) // GLA_DK
    t_i = lax.broadcasted_iota(jnp.int32, (GLA_CHUNK, GLA_CHUNK), 0)
    s_i = lax.broadcasted_iota(jnp.int32, (GLA_CHUNK, GLA_CHUNK), 1)
    causal = s_i <= t_i
    o_rows = []
    for c in range(tm // GLA_CHUNK):
        r0 = c * GLA_CHUNK
        bc = bcum[r0:r0 + GLA_CHUNK]
        bref = bc[GLA_CHUNK // 2:GLA_CHUNK // 2 + 1]
        blast = bc[GLA_CHUNK - 1:GLA_CHUNK]
        qc = gq[r0:r0 + GLA_CHUNK]
        kc = gk[r0:r0 + GLA_CHUNK]
        vc = gv[r0:r0 + GLA_CHUNK]
        qe = qc * jnp.exp(bc)
        qt = qc * jnp.exp(bc - bref)
        kt = (kc * jnp.exp(bref - bc)).astype(BF16)
        kd = kc * jnp.exp(blast - bc)
        s2 = s2_scr[...]
        s2b = s2.astype(BF16)
        upd = jnp.zeros_like(s2)
        o_heads = []
        for h in range(GLA_HEADS):
            mh = head_of_lane == h
            att = _nt(jnp.where(mh, qt, 0.0).astype(BF16), kt)
            att = jnp.where(causal, att, 0.0).astype(BF16)
            vh = vc[:, h * GLA_DV:(h + 1) * GLA_DV]
            o_h = _nn(att, vh) + _nt(jnp.where(mh, qe, 0.0).astype(BF16), s2b)
            o_heads.append(o_h)
            upd = upd + _tn(vh, jnp.where(mh, kd, 0.0).astype(BF16))
        s2_scr[...] = s2 * jnp.exp(blast) + upd
        o_rows.append(jnp.concatenate(o_heads, axis=1))
    o_c = jnp.concatenate(o_rows, axis=0)
    st_ref[0] = s2_scr[...]
    o_n = jnp.concatenate(
        [_ln(o_c[:, h * GLA_DV:(h + 1) * GLA_DV], gng_ref[:, h * GLA_DV:(h + 1) * GLA_DV],
             gnb_ref[:, h * GLA_DV:(h + 1) * GLA_DV]) for h in range(GLA_HEADS)], axis=1)
    y_c = jax.nn.silu(proj(C_GR, C_GA)) * o_n

    g_a = jax.nn.sigmoid(proj(C_GATE, C_GATE + D_MODEL))
    g_c = jax.nn.sigmoid(proj(C_GATE + 2 * D_MODEL, C_GATE + 3 * D_MODEL))
    mac_ref[0] = g_a * _nn(y_a.astype(BF16), wbra_ref[...]) + g_c * _nn(y_c.astype(BF16), wbrc_ref[...])
    gb_ref[0] = jax.nn.sigmoid(proj(C_GATE + D_MODEL, C_GATE + 2 * D_MODEL))


def _p1_call(x, win, bin_, ws, bsf, gmg, gmb, wa2, ba, gng, gnb, wbra, wbrc, tri):
    bsz, seq, _ = x.shape
    tm = TM1
    tile = lambda w: pl.BlockSpec((1, tm, w), lambda b, s: (b, s, 0))
    consts = (win, bin_, ws, bsf, gmg, gmb, wa2, ba, gng, gnb, wbra, wbrc, tri)
    out_shape = (
        jax.ShapeDtypeStruct((bsz, seq, MIX_W), F32), jax.ShapeDtypeStruct((bsz, seq, MIX_W), F32),
        jax.ShapeDtypeStruct((bsz, seq, MIX_W), BF16), jax.ShapeDtypeStruct((bsz, seq, MIX_W), BF16),
        jax.ShapeDtypeStruct((bsz, seq, MIX_W), BF16),
        jax.ShapeDtypeStruct((bsz, seq, D_MODEL), F32), jax.ShapeDtypeStruct((bsz, seq, D_MODEL), F32),
        jax.ShapeDtypeStruct((bsz, GLA_DV, GLA_KW), F32),
    )
    out_specs = (tile(MIX_W), tile(MIX_W), tile(MIX_W), tile(MIX_W), tile(MIX_W), tile(D_MODEL), tile(D_MODEL),
                 pl.BlockSpec((1, GLA_DV, GLA_KW), lambda b, s: (b, 0, 0)))
    return pl.pallas_call(
        _p1_kernel,
        grid=(bsz, seq // tm),
        in_specs=[tile(D_MODEL)] + [_const_spec(c.shape) for c in consts],
        out_specs=out_specs,
        out_shape=out_shape,
        scratch_shapes=[pltpu.VMEM((GLA_DV, GLA_KW), F32)],
        compiler_params=pltpu.CompilerParams(dimension_semantics=("arbitrary", "arbitrary"),
                                             vmem_limit_bytes=VMEM_LIMIT),
        name="prompt_proj_mixers",
    )(x, *consts)


def _p2_kernel(bias_ref, q_ref, k_ref, v_ref, uo_ref, o_ref):
    tq = q_ref.shape[1]
    tk = TK
    hp = pl.program_id(1)
    i = pl.program_id(2)
    q = q_ref[0]
    lane = lax.broadcasted_iota(jnp.int32, q.shape, 1)
    t_i = lax.broadcasted_iota(jnp.int32, (tq, tk), 0)
    s_i = lax.broadcasted_iota(jnp.int32, (tq, tk), 1)
    causal = s_i < t_i
    uo = uo_ref[...]

    def block(j, qh, bias, carry, diag):
        c, acc = carry
        start = pl.multiple_of(j * tk, tk)
        kblk = k_ref[0, pl.ds(start, tk), :]
        vblk = v_ref[0, pl.ds(start, tk), :]
        z = _nt(qh, kblk) + bias
        sp = _softplus(z)
        log_keep = -sp
        if diag:
            log_keep = jnp.where(causal, log_keep, 0.0)
        hi, lo = _split_bf16(log_keep)
        r2 = _nn(hi, uo) + _nn(lo, uo)
        w = jnp.exp((z - sp) + (c + r2[:, :tk]))
        if diag:
            w = jnp.where(causal, w, 0.0)
        acc = acc + _nn(w.astype(BF16), vblk)
        return c + r2[:, tk:], acc

    outs = []
    for hh in range(2):
        qh = jnp.where((lane < SB_HD) if hh == 0 else (lane >= SB_HD), q, jnp.zeros_like(q))
        bias = bias_ref[2 * hp + hh]
        carry = (jnp.zeros((tq, tk), F32), jnp.zeros((tq, LANES), F32))
        carry = block(i, qh, bias, carry, True)
        carry = lax.fori_loop(0, i, lambda it, cr: block(i - 1 - it, qh, bias, cr, False), carry)
        outs.append(carry[1])
    o_ref[0] = jnp.where(lane < SB_HD, outs[0], outs[1]).astype(o_ref.dtype)


def _p2_call(bias, qb, kb, vb, uo):
    bsz, seq, _ = qb.shape
    n_hp = MIX_W // LANES
    return pl.pallas_call(
        _p2_kernel,
        grid=(bsz, n_hp, seq // TQ),
        in_specs=[pl.BlockSpec(memory_space=pltpu.SMEM),
                  pl.BlockSpec((1, TQ, LANES), lambda b, h, i: (b, i, h)),
                  pl.BlockSpec((1, seq, LANES), lambda b, h, i: (b, 0, h)),
                  pl.BlockSpec((1, seq, LANES), lambda b, h, i: (b, 0, h)),
                  pl.BlockSpec(uo.shape, lambda b, h, i: (0, 0))],
        out_specs=pl.BlockSpec((1, TQ, LANES), lambda b, h, i: (b, i, h)),
        out_shape=jax.ShapeDtypeStruct((bsz, seq, MIX_W), BF16),
        compiler_params=pltpu.CompilerParams(dimension_semantics=("arbitrary", "arbitrary", "arbitrary"),
                                             vmem_limit_bytes=VMEM_LIMIT),
        name="prompt_stick_breaking",
    )(bias, qb, kb, vb, uo)


def _channel_mixer_tail(x1, f, pb, wple_ref, wpg_ref, ln2g_ref, ln2b_ref):
    x1b = x1.astype(BF16)
    e = _nn(pb, wple_ref[...]) * jax.nn.sigmoid(_nn(x1b, wpg_ref[...]))
    return _ln(DN_ALPHA * x1 + f + e, ln2g_ref[...], ln2b_ref[...])


def _p3_kernel(x_ref, mac_ref, gb_ref, yb_ref, p_ref, wbrb_ref, wout_ref, wup_ref, wdown_ref, wple_ref, wpg_ref,
               ln1g_ref, ln1b_ref, cw_ref, cb_ref, ln2g_ref, ln2b_ref,
               x2_ref, conv_ref, hs_scr):
    tm = x_ref.shape[1]
    pad = SUBLANES

    @pl.when(pl.program_id(1) == 0)
    def _():
        hs_scr[0:pad, :] = jnp.zeros((pad, D2), F32)

    merged = mac_ref[0] + gb_ref[0] * _nn(yb_ref[0], wbrb_ref[...])
    h = _nn(merged.astype(BF16), wout_ref[...])
    x1 = _ln(DN_ALPHA * x_ref[0] + h, ln1g_ref[...], ln1b_ref[...])
    x1b = x1.astype(BF16)

    f = jnp.zeros((tm, D_MODEL), F32)
    for c0 in range(0, D_FF, FF_CH):
        halves = []
        for base in (c0, D_FF + c0):
            hs_scr[pad:pad + tm, base:base + FF_CH] = _nn(x1b, wup_ref[:, base:base + FF_CH])
            hc = cb_ref[:, base:base + FF_CH]
            for i in range(CONV_W):
                hc = hc + cw_ref[i:i + 1, base:base + FF_CH] * hs_scr[pad - 2 + i:pad - 2 + i + tm, base:base + FF_CH]
            halves.append(hc)
        act = (jax.nn.gelu(halves[0]) * halves[1]).astype(BF16)
        f = f + _nn(act, wdown_ref[c0:c0 + FF_CH, :])
    last = hs_scr[pad + tm - 2:pad + tm, :]
    hs_scr[pad - 2:pad, :] = last
    conv_ref[0] = last

    x2_ref[0] = _channel_mixer_tail(x1, f, p_ref[0].astype(BF16), wple_ref, wpg_ref, ln2g_ref, ln2b_ref)


def _p3_call(layer, x, mac, gb, yb, p_all, wbrb, wout, wup, wdown, wple, wpg, ln1g, ln1b, cw, cb, ln2g, ln2b):
    bsz, seq, _ = x.shape
    tm = TM3
    tile = lambda w: pl.BlockSpec((1, tm, w), lambda b, s: (b, s, 0))
    consts = (wbrb, wout, wup, wdown, wple, wpg, ln1g, ln1b, cw, cb, ln2g, ln2b)
    return pl.pallas_call(
        _p3_kernel,
        grid=(bsz, seq // tm),
        in_specs=[tile(D_MODEL), tile(D_MODEL), tile(D_MODEL), tile(MIX_W),
                  pl.BlockSpec((None, 1, tm, PLE_DIM), lambda b, s: (layer, b, s, 0))]
                 + [_const_spec(c.shape) for c in consts],
        out_specs=(tile(D_MODEL), pl.BlockSpec((1, CONV_W - 1, D2), lambda b, s: (b, 0, 0))),
        out_shape=(jax.ShapeDtypeStruct((bsz, seq, D_MODEL), F32),
                   jax.ShapeDtypeStruct((bsz, CONV_W - 1, D2), F32)),
        scratch_shapes=[pltpu.VMEM((tm + SUBLANES, D2), F32)],
        compiler_params=pltpu.CompilerParams(dimension_semantics=("arbitrary", "arbitrary"),
                                             vmem_limit_bytes=VMEM_LIMIT),
        name="prompt_channel_mixer",
    )(x, mac, gb, yb, p_all, *consts)


def _d1_kernel(x_ref, win_ref, bin_ref, ws0_ref, bs0_ref, gmg_ref, gmb_ref, wa2_ref, ba_ref, gng_ref, gnb_ref,
               wbra_ref, wbrc_ref, st_ref,
               kf_ref, vf_ref, qf_ref, gmv_ref, mac_ref, gb_ref, stn_ref, oc_scr):
    nb = x_ref.shape[0]
    xb = x_ref[...].astype(BF16)

    def proj(c0, c1):
        return _nn(xb, win_ref[:, c0:c1]) + bin_ref[:, c0:c1]

    qf_ref[...] = proj(C_Q, C_K) * (SB_HD ** -0.5)
    kf_ref[...] = proj(C_K, C_VV)
    vf_ref[...] = proj(C_VV, C_GQ)

    u = jax.nn.gelu(proj(C_U, C_V))
    vn = _ln(jax.nn.gelu(proj(C_V, C_Q)), gmg_ref[...], gmb_ref[...])
    gmv_ref[...] = vn
    y_a = u * (ws0_ref[...] * vn + bs0_ref[...])

    gq = proj(C_GQ, C_GK) * (GLA_DK ** -0.5)
    gk = proj(C_GK, C_GV)
    gv = proj(C_GV, C_GR)
    ga = proj(C_GA, C_GATE).astype(BF16)
    a = jnp.exp(_log_sigmoid(_nn(ga, wa2_ref[...]) + ba_ref[...]) * (1.0 / GLA_TAU))

    def cols(m):
        return jnp.concatenate([m, jnp.zeros((LANES - nb, GLA_KW), F32)], axis=0).T

    a_t, k_t, q_t = cols(a), cols(gk), cols(gq)
    for b in range(nb):
        s_old = st_ref[b]
        v_exp = jnp.concatenate(
            [jnp.broadcast_to(gv[b:b + 1, h * GLA_DV:(h + 1) * GLA_DV], (GLA_DK, GLA_DV)) for h in range(GLA_HEADS)],
            axis=0)
        s_new = a_t[:, b:b + 1] * s_old + k_t[:, b:b + 1] * v_exp
        stn_ref[b] = s_new
        oq = q_t[:, b:b + 1] * s_new
        for h in range(GLA_HEADS):
            oc_scr[b:b + 1, h * GLA_DV:(h + 1) * GLA_DV] = jnp.sum(oq[h * GLA_DK:(h + 1) * GLA_DK], axis=0,
                                                                     keepdims=True)
    o_c = oc_scr[...]
    o_n = jnp.concatenate(
        [_ln(o_c[:, h * GLA_DV:(h + 1) * GLA_DV], gng_ref[:, h * GLA_DV:(h + 1) * GLA_DV],
             gnb_ref[:, h * GLA_DV:(h + 1) * GLA_DV]) for h in range(GLA_HEADS)], axis=1)
    y_c = jax.nn.silu(proj(C_GR, C_GA)) * o_n

    g_a = jax.nn.sigmoid(proj(C_GATE, C_GATE + D_MODEL))
    g_c = jax.nn.sigmoid(proj(C_GATE + 2 * D_MODEL, C_GATE + 3 * D_MODEL))
    mac_ref[...] = g_a * _nn(y_a.astype(BF16), wbra_ref[...]) + g_c * _nn(y_c.astype(BF16), wbrc_ref[...])
    gb_ref[...] = jax.nn.sigmoid(proj(C_GATE + D_MODEL, C_GATE + 2 * D_MODEL))


def _d1_call(x, win, bin_, ws0, bs0, gmg, gmb, wa2, ba, gng, gnb, wbra, wbrc, st):
    nb = x.shape[0]
    out_shape = (
        jax.ShapeDtypeStruct((nb, MIX_W), F32), jax.ShapeDtypeStruct((nb, MIX_W), F32),
        jax.ShapeDtypeStruct((nb, MIX_W), F32), jax.ShapeDtypeStruct((nb, MIX_W), F32),
        jax.ShapeDtypeStruct((nb, D_MODEL), F32), jax.ShapeDtypeStruct((nb, D_MODEL), F32),
        jax.ShapeDtypeStruct((nb, GLA_KW, GLA_DV), F32),
    )
    return pl.pallas_call(
        _d1_kernel,
        out_shape=out_shape,
        scratch_shapes=[pltpu.VMEM((nb, MIX_W), F32)],
        compiler_params=pltpu.CompilerParams(vmem_limit_bytes=VMEM_LIMIT),
        name="sample_proj_mixers",
    )(x, win, bin_, ws0, bs0, gmg, gmb, wa2, ba, gng, gnb, wbra, wbrc, st)


def _d2_kernel(pt_ref, qall_ref, bias_ref, msuf_ref, glater_ref, gall_ref, e_ref, *refs):
    n = PAGES_PER_STEP
    k_refs, v_refs = refs[:n], refs[n:2 * n]
    o_ref, acc_scr, c_scr = refs[2 * n], refs[2 * n + 1], refs[2 * n + 2]
    j = pl.program_id(1)

    @pl.when(j == 0)
    def _():
        acc_scr[...] = jnp.zeros_like(acc_scr)
        c_scr[...] = jnp.zeros_like(c_scr)

    z = bias_ref[...] + _nn(k_refs[0][0].astype(BF16), qall_ref[0, 0])
    for i in range(1, n):
        z = z + _nn(k_refs[i][0].astype(BF16), qall_ref[0, i])
    sp = _softplus(z)
    log_keep = -sp
    hi, lo = _split_bf16(log_keep)
    msuf = msuf_ref[...]
    r = _nn(msuf, hi) + _nn(msuf, lo)
    tot = jnp.broadcast_to(r[0:1] + log_keep[0:1], (SUBLANES, LANES))
    t_hi, t_lo = _split_bf16(tot)
    later = _nn(t_hi, glater_ref[...]) + _nn(t_lo, glater_ref[...])
    c = c_scr[...]
    w = jnp.exp((z - sp) + r + (c + later)[0:1]).astype(BF16)
    acc = acc_scr[...]
    for i in range(n):
        acc = acc + _nn(w, e_ref[i]) * v_refs[i][0]
    acc_scr[...] = acc
    c_scr[...] = c + _nn(t_hi, gall_ref[...]) + _nn(t_lo, gall_ref[...])

    @pl.when(j == pl.num_programs(1) - 1)
    def _():
        o_ref[0] = jnp.sum(acc, axis=0, keepdims=True)


def _d2_call(layer, page_table, qall, bias_lane, msuf, glater, gall, e_mat, cache_k, cache_v):
    nb, n_pages = page_table.shape
    n = PAGES_PER_STEP
    steps = n_pages // n

    def page_spec(i):
        return pl.BlockSpec((None, 1, PAGE, MIX_W),
                            lambda b, j, pt: (layer, pt[b, n_pages - 1 - (j * n + i)], 0, 0))

    const = lambda shape: pl.BlockSpec(shape, lambda b, j, pt: (0,) * len(shape))
    grid_spec = pltpu.PrefetchScalarGridSpec(
        num_scalar_prefetch=1,
        grid=(nb, steps),
        in_specs=[pl.BlockSpec((1, n, MIX_W, LANES), lambda b, j, pt: (b, 0, 0, 0)),
                  const(bias_lane.shape), const(msuf.shape), const(glater.shape), const(gall.shape),
                  const(e_mat.shape)]
                 + [page_spec(i) for i in range(n)] + [page_spec(i) for i in range(n)],
        out_specs=pl.BlockSpec((1, 1, MIX_W), lambda b, j, pt: (b, 0, 0)),
        scratch_shapes=[pltpu.VMEM((PAGE, MIX_W), F32), pltpu.VMEM((SUBLANES, LANES), F32)],
    )
    return pl.pallas_call(
        _d2_kernel,
        grid_spec=grid_spec,
        out_shape=jax.ShapeDtypeStruct((nb, 1, MIX_W), F32),
        compiler_params=pltpu.CompilerParams(dimension_semantics=("arbitrary", "arbitrary"),
                                             vmem_limit_bytes=VMEM_LIMIT),
        name="sample_paged_stick_breaking",
    )(page_table, qall, bias_lane, msuf, glater, gall, e_mat, *([cache_k] * n), *([cache_v] * n))


def _d3_kernel(x_ref, mac_ref, gb_ref, yb_ref, p_ref, prev0_ref, prev1_ref, wbrb_ref, wout_ref, wup_ref, wdown_ref,
               wple_ref, wpg_ref, ln1g_ref, ln1b_ref, cw_ref, cb_ref, ln2g_ref, ln2b_ref,
               x2_ref, hup_ref):
    nb = x_ref.shape[0]
    merged = mac_ref[...] + gb_ref[...] * _nn(yb_ref[...].astype(BF16), wbrb_ref[...])
    h = _nn(merged.astype(BF16), wout_ref[...])
    x1 = _ln(DN_ALPHA * x_ref[...] + h, ln1g_ref[...], ln1b_ref[...])
    x1b = x1.astype(BF16)
    f = jnp.zeros((nb, D_MODEL), F32)
    for c0 in range(0, D_FF, FF_CH):
        halves = []
        for base in (c0, D_FF + c0):
            sl = slice(base, base + FF_CH)
            hup = _nn(x1b, wup_ref[:, sl])
            hup_ref[:, sl] = hup
            halves.append(cb_ref[:, sl] + cw_ref[0:1, sl] * prev0_ref[:, sl] + cw_ref[1:2, sl] * prev1_ref[:, sl]
                          + cw_ref[2:3, sl] * hup)
        act = (jax.nn.gelu(halves[0]) * halves[1]).astype(BF16)
        f = f + _nn(act, wdown_ref[c0:c0 + FF_CH, :])
    x2_ref[...] = _channel_mixer_tail(x1, f, p_ref[...].astype(BF16), wple_ref, wpg_ref, ln2g_ref, ln2b_ref)


def _d3_call(x, mac, gb, yb, p, prev0, prev1, wbrb, wout, wup, wdown, wple, wpg, ln1g, ln1b, cw, cb, ln2g, ln2b):
    nb = x.shape[0]
    return pl.pallas_call(
        _d3_kernel,
        out_shape=(jax.ShapeDtypeStruct((nb, D_MODEL), F32), jax.ShapeDtypeStruct((nb, D2), F32)),
        compiler_params=pltpu.CompilerParams(vmem_limit_bytes=VMEM_LIMIT),
        name="sample_channel_mixer",
    )(x, mac, gb, yb, p, prev0, prev1, wbrb, wout, wup, wdown, wple, wpg, ln1g, ln1b, cw, cb, ln2g, ln2b)


def _constants():
    tri = np.zeros((TM1, TM1), np.float32)
    for c in range(TM1 // GLA_CHUNK):
        r0 = c * GLA_CHUNK
        tri[r0:r0 + GLA_CHUNK, r0:r0 + GLA_CHUNK] = np.tril(np.ones((GLA_CHUNK, GLA_CHUNK), np.float32))
    jj = np.arange(TK)
    uo = np.concatenate([(jj[:, None] > jj[None, :]).astype(np.float32), np.ones((TK, TK), np.float32)], axis=1)
    kk = np.arange(PAGE)
    msuf = (kk[None, :] > kk[:, None]).astype(np.float32)
    ln = np.arange(LANES)
    page_of, head_of, used = ln // SB_HEADS, ln % SB_HEADS, ln < PAGES_PER_STEP * SB_HEADS
    same_head = (head_of[:, None] == head_of[None, :]) & used[:, None] & used[None, :]
    glater = (same_head & (page_of[:, None] < page_of[None, :])).astype(np.float32)
    gall = same_head.astype(np.float32)
    col_head = np.arange(MIX_W) // SB_HD
    e_mat = np.stack([((page_of[:, None] == i) & used[:, None] & (head_of[:, None] == col_head[None, :]))
                      .astype(np.float32) for i in range(PAGES_PER_STEP)])
    as_bf16 = lambda m: jnp.asarray(m, BF16)
    return as_bf16(tri), as_bf16(uo), as_bf16(msuf), as_bf16(glater), as_bf16(gall), as_bf16(e_mat)


def _row(vec):
    return vec.reshape(1, -1)


def kernel(x_prompt, x_sample, cache_k, cache_v, state_gla, state_conv, page_table, p_prompt, p_sample, w_in, b_in, gm_ws, gm_bs, gm_ln_g, gm_ln_b, sb_bias, gla_wa2, gla_ba, gla_gn_g, gla_gn_b, w_br_a, w_br_b, w_br_c, w_out, ln1_g, ln1_b, w_up, conv_w, conv_b, w_down, w_ple, w_pgate, ln2_g, ln2_b):
    bp, sp = x_prompt.shape[0], x_prompt.shape[1]
    bd = x_sample.shape[0]
    n_pool = cache_k.shape[1]
    tri, uo, msuf, glater, gall, e_mat = _constants()
    ck = cache_k.reshape(DEPTH, n_pool, PAGE, MIX_W)
    cv = cache_v.reshape(DEPTH, n_pool, PAGE, MIX_W)
    lane = jnp.arange(LANES)
    hd = jnp.arange(MIX_W)

    yp = x_prompt
    yd = x_sample.reshape(bd, D_MODEL)
    outs = [[] for _ in range(9)]
    for i in range(DEPTH):
        ga_pad = jnp.zeros((D_MODEL, LANES - GLA_RANK), F32)
        win = jnp.concatenate([w_in[i][:, :C_GA], w_in[i][:, C_GA:C_GA + GLA_RANK], ga_pad,
                               w_in[i][:, C_GA + GLA_RANK:]], axis=1).astype(BF16)
        bin_ = jnp.concatenate([b_in[i][:C_GA + GLA_RANK], jnp.zeros((LANES - GLA_RANK,), F32),
                                b_in[i][C_GA + GLA_RANK:]]).reshape(1, D_INP)
        wa2 = jnp.concatenate([gla_wa2[i], jnp.zeros((LANES - GLA_RANK, GLA_KW), F32)], axis=0).astype(BF16)
        bsf = jnp.repeat(gm_bs[i].T, LANES, axis=1)
        ws0 = _row(jnp.repeat(gm_ws[i][:, 0, 0], LANES))
        bs0 = _row(jnp.repeat(gm_bs[i][:, 0], LANES))
        gmg, gmb, ba = _row(gm_ln_g[i]), _row(gm_ln_b[i]), _row(gla_ba[i])
        gng, gnb = _row(gla_gn_g[i]), _row(gla_gn_b[i])
        wbra, wbrb, wbrc = w_br_a[i].astype(BF16), w_br_b[i].astype(BF16), w_br_c[i].astype(BF16)
        wout, wup, wdown = w_out[i].astype(BF16), w_up[i].astype(BF16), w_down[i].astype(BF16)
        wple, wpg = w_ple[i].astype(BF16), w_pgate[i].astype(BF16)
        ln1g, ln1b, ln2g, ln2b = _row(ln1_g[i]), _row(ln1_b[i]), _row(ln2_g[i]), _row(ln2_b[i])
        cw, cb = conv_w[i], _row(conv_b[i])

        kf, vf, qb, kb, vb, mac, gb, s2 = _p1_call(yp, win, bin_, gm_ws[i], bsf, gmg, gmb, wa2, ba, gng, gnb,
                                                    wbra, wbrc, tri)
        yb = _p2_call(sb_bias[i], qb, kb, vb, uo)
        yp, conv_p = _p3_call(i, yp, mac, gb, yb, p_prompt, wbrb, wout, wup, wdown, wple, wpg,
                              ln1g, ln1b, cw, cb, ln2g, ln2b)
        gla_p = s2.reshape(bp, GLA_DV, GLA_HEADS, GLA_DK).transpose(0, 2, 3, 1)

        st = state_gla[i].reshape(bd, GLA_KW, GLA_DV)
        kd, vd, qd, gmv, mac_d, gb_d, st_new = _d1_call(yd, win, bin_, ws0, bs0, gmg, gmb, wa2, ba, gng, gnb,
                                                         wbra, wbrc, st)
        sel = lane[None, None, :] == (jnp.arange(PAGES_PER_STEP)[:, None, None] * SB_HEADS
                                      + (hd // SB_HD)[None, :, None])
        qall = jnp.where(sel[None], qd[:, None, :, None], 0.0).astype(BF16)
        bias_lane = _row(jnp.where(lane < PAGES_PER_STEP * SB_HEADS, sb_bias[i][lane % SB_HEADS], 0.0))
        yb_d = _d2_call(i, page_table, qall, bias_lane, msuf, glater, gall, e_mat, ck, cv).reshape(bd, MIX_W)
        yd, hup_d = _d3_call(yd, mac_d, gb_d, yb_d, p_sample[i].reshape(bd, PLE_DIM),
                             state_conv[i][:, 0], state_conv[i][:, 1],
                             wbrb, wout, wup, wdown, wple, wpg, ln1g, ln1b, cw, cb, ln2g, ln2b)

        outs[0].append(kf.reshape(bp, sp, SB_HEADS, SB_HD))
        outs[1].append(vf.reshape(bp, sp, SB_HEADS, SB_HD))
        outs[2].append(kd.reshape(bd, 1, SB_HEADS, SB_HD))
        outs[3].append(vd.reshape(bd, 1, SB_HEADS, SB_HD))
        outs[4].append(gla_p)
        outs[5].append(st_new.reshape(bd, GLA_HEADS, GLA_DK, GLA_DV))
        outs[6].append(conv_p)
        outs[7].append(jnp.stack([state_conv[i][:, 1], hup_d], axis=1))
        outs[8].append(gmv.reshape(bd, 1, MIX_W))

    return (yp, yd.reshape(bd, 1, D_MODEL)) + tuple(jnp.stack(o) for o in outs)
```

```python
import functools

import numpy as np
import jax
import jax.numpy as jnp
from jax import lax
from jax.experimental import pallas as pl
from jax.experimental.pallas import tpu as pltpu

F32 = jnp.float32
BF16 = jnp.bfloat16

D_MODEL = 1024
DEPTH = 2
PAGE = 128
MIX_W = 512
GM_CHUNK = 128
GM_GROUPS = 4
SB_HEADS = 8
SB_HD = 64
GLA_HEADS = 4
GLA_DK = 64
GLA_DV = 128
GLA_KW = 256
GLA_RANK = 16
GLA_TAU = 16.0
GLA_CHUNK = 64
D_FF = 2816
D2 = 2 * D_FF
CONV_W = 3
PLE_DIM = 256
LN_EPS = 1e-5
DN_ALPHA = (2.0 * DEPTH) ** 0.25
LANES = 128
SUBLANES = 8

C_U, C_V, C_Q, C_K, C_VV = 0, 512, 1024, 1536, 2048
C_GQ, C_GK, C_GV, C_GR, C_GA, C_GATE = 2560, 2816, 3072, 3584, 4096, 4224
D_INP = C_GATE + 3 * D_MODEL

TM1 = 256
TM3 = 256
TQ = 512
TK = 256
SB_SLABS = 2
FF_CH = 256
PAGES_PER_STEP = 8
VMEM_LIMIT = 56 * 1024 * 1024


def _nt(a, b):
    return lax.dot_general(a, b, (((1,), (1,)), ((), ())), preferred_element_type=F32)


def _tn(a, b):
    return lax.dot_general(a, b, (((0,), (0,)), ((), ())), preferred_element_type=F32)


def _nn(a, b):
    return jnp.dot(a, b, preferred_element_type=F32)


def _ln(x, g, b):
    mu = jnp.mean(x, axis=-1, keepdims=True)
    xc = x - mu
    var = jnp.mean(xc * xc, axis=-1, keepdims=True)
    return xc * lax.rsqrt(var + LN_EPS) * g + b


def _softplus(z):
    return jnp.maximum(z, 0.0) + jnp.log1p(jnp.exp(-jnp.abs(z)))


def _log_sigmoid(z):
    return jnp.minimum(z, 0.0) - jnp.log1p(jnp.exp(-jnp.abs(z)))


def _split_bf16(x):
    hi = x.astype(BF16)
    lo = (x - hi.astype(F32)).astype(BF16)
    return hi, lo


def _const_spec(shape):
    nd = len(shape)
    return pl.BlockSpec(shape, lambda *_: (0,) * nd, pipeline_mode=pl.Buffered(1))


def _p1_kernel(x_ref, win_ref, bin_ref, ws_ref, bsf_ref, gmg_ref, gmb_ref, wa2_ref, ba_ref, gng_ref, gnb_ref,
               wbra_ref, wbrc_ref, tri_ref,
               kf_ref, vf_ref, qb_ref, kb_ref, vb_ref, mac_ref, gb_ref, st_ref, s2_scr):
    tm = x_ref.shape[1]

    @pl.when(pl.program_id(1) == 0)
    def _():
        s2_scr[...] = jnp.zeros_like(s2_scr)

    xb = x_ref[0].astype(BF16)

    def proj(c0, c1):
        return _nn(xb, win_ref[:, c0:c1]) + bin_ref[:, c0:c1]

    q = proj(C_Q, C_K)
    k = proj(C_K, C_VV)
    v = proj(C_VV, C_GQ)
    kf_ref[0] = k
    vf_ref[0] = v
    qb_ref[0] = (q * (SB_HD ** -0.5)).astype(BF16)
    kb_ref[0] = k.astype(BF16)
    vb_ref[0] = v.astype(BF16)

    u = jax.nn.gelu(proj(C_U, C_V))
    vn = _ln(jax.nn.gelu(proj(C_V, C_Q)), gmg_ref[...], gmb_ref[...])
    vnb = vn.astype(BF16)
    r_i = lax.broadcasted_iota(jnp.int32, (GM_CHUNK, GM_CHUNK), 0)
    c_i = lax.broadcasted_iota(jnp.int32, (GM_CHUNK, GM_CHUNK), 1)
    ws_m = [jnp.where(r_i >= c_i, ws_ref[g], 0.0).astype(BF16) for g in range(GM_GROUPS)]
    sv_rows = []
    for n in range(tm // GM_CHUNK):
        cols = [_nn(ws_m[g], vnb[n * GM_CHUNK:(n + 1) * GM_CHUNK, g * LANES:(g + 1) * LANES])
                for g in range(GM_GROUPS)]
        sv_rows.append(jnp.concatenate(cols, axis=1) + bsf_ref[...])
    y_a = u * jnp.concatenate(sv_rows, axis=0)

    gq = proj(C_GQ, C_GK) * (GLA_DK ** -0.5)
    gk = proj(C_GK, C_GV)
    gv = proj(C_GV, C_GR).astype(BF16)
    ga = proj(C_GA, C_GATE).astype(BF16)
    la = _log_sigmoid(_nn(ga, wa2_ref[...]) + ba_ref[...]) * (1.0 / GLA_TAU)
    la_hi, la_lo = _split_bf16(la)
    tri = tri_ref[...]
    bcum = _nn(tri, la_hi) + _nn(tri, la_lo)
    head_of_lane = lax.broadcasted_iota(jnp.int32, (GLA_CHUNK, GLA_KW), 1) // GLA_DK
    t_i = lax.broadcasted_iota(jnp.int32, (GLA_CHUNK, GLA_CHUNK), 0)
    s_i = lax.broadcasted_iota(jnp.int32, (GLA_CHUNK, GLA_CHUNK), 1)
    causal = s_i <= t_i
    o_rows = []
    for c in range(tm // GLA_CHUNK):
        r0 = c * GLA_CHUNK
        bc = bcum[r0:r0 + GLA_CHUNK]
        bref = bc[GLA_CHUNK // 2:GLA_CHUNK // 2 + 1]
        blast = bc[GLA_CHUNK - 1:GLA_CHUNK]
        qc = gq[r0:r0 + GLA_CHUNK]
        kc = gk[r0:r0 + GLA_CHUNK]
        vc = gv[r0:r0 + GLA_CHUNK]
        qe = qc * jnp.exp(bc)
        qt = qc * jnp.exp(bc - bref)
        kt = (kc * jnp.exp(bref - bc)).astype(BF16)
        kd = kc * jnp.exp(blast - bc)
        s2 = s2_scr[...]
        s2b = s2.astype(BF16)
        upd = jnp.zeros_like(s2)
        o_heads = []
        for h in range(GLA_HEADS):
            mh = head_of_lane == h
            att = _nt(jnp.where(mh, qt, 0.0).astype(BF16), kt)
            att = jnp.where(causal, att, 0.0).astype(BF16)
            vh = vc[:, h * GLA_DV:(h + 1) * GLA_DV]
            o_h = _nn(att, vh) + _nt(jnp.where(mh, qe, 0.0).astype(BF16), s2b)
            o_heads.append(o_h)
            upd = upd + _tn(vh, jnp.where(mh, kd, 0.0).astype(BF16))
        s2_scr[...] = s2 * jnp.exp(blast) + upd
        o_rows.append(jnp.concatenate(o_heads, axis=1))
    o_c = jnp.concatenate(o_rows, axis=0)
    st_ref[0] = s2_scr[...]
    o_n = jnp.concatenate(
        [_ln(o_c[:, h * GLA_DV:(h + 1) * GLA_DV], gng_ref[:, h * GLA_DV:(h + 1) * GLA_DV],
             gnb_ref[:, h * GLA_DV:(h + 1) * GLA_DV]) for h in range(GLA_HEADS)], axis=1)
    y_c = jax.nn.silu(proj(C_GR, C_GA)) * o_n

    g_a = jax.nn.sigmoid(proj(C_GATE, C_GATE + D_MODEL))
    g_c = jax.nn.sigmoid(proj(C_GATE + 2 * D_MODEL, C_GATE + 3 * D_MODEL))
    mac_ref[0] = g_a * _nn(y_a.astype(BF16), wbra_ref[...]) + g_c * _nn(y_c.astype(BF16), wbrc_ref[...])
    gb_ref[0] = jax.nn.sigmoid(proj(C_GATE + D_MODEL, C_GATE + 2 * D_MODEL))


def _p1_call(x, win, bin_, ws, bsf, gmg, gmb, wa2, ba, gng, gnb, wbra, wbrc, tri):
    bsz, seq, _ = x.shape
    tm = TM1
    tile = lambda w: pl.BlockSpec((1, tm, w), lambda b, s: (b, s, 0))
    consts = (win, bin_, ws, bsf, gmg, gmb, wa2, ba, gng, gnb, wbra, wbrc, tri)
    out_shape = (
        jax.ShapeDtypeStruct((bsz, seq, MIX_W), F32), jax.ShapeDtypeStruct((bsz, seq, MIX_W), F32),
        jax.ShapeDtypeStruct((bsz, seq, MIX_W), BF16), jax.ShapeDtypeStruct((bsz, seq, MIX_W), BF16),
        jax.ShapeDtypeStruct((bsz, seq, MIX_W), BF16),
        jax.ShapeDtypeStruct((bsz, seq, D_MODEL), F32), jax.ShapeDtypeStruct((bsz, seq, D_MODEL), F32),
        jax.ShapeDtypeStruct((bsz, GLA_DV, GLA_KW), F32),
    )
    out_specs = (tile(MIX_W), tile(MIX_W), tile(MIX_W), tile(MIX_W), tile(MIX_W), tile(D_MODEL), tile(D_MODEL),
                 pl.BlockSpec((1, GLA_DV, GLA_KW), lambda b, s: (b, 0, 0)))
    return pl.pallas_call(
        _p1_kernel,
        grid=(bsz, seq // tm),
        in_specs=[tile(D_MODEL)] + [_const_spec(c.shape) for c in consts],
        out_specs=out_specs,
        out_shape=out_shape,
        scratch_shapes=[pltpu.VMEM((GLA_DV, GLA_KW), F32)],
        compiler_params=pltpu.CompilerParams(dimension_semantics=("arbitrary", "arbitrary"),
                                             vmem_limit_bytes=VMEM_LIMIT),
        name="prompt_proj_mixers",
    )(x, *consts)


def _p2_kernel(bias_ref, q_ref, k_ref, v_ref, u2_ref, o_ref):
    tq = q_ref.shape[1]
    tk = TK
    rs = tq // SB_SLABS
    n_diag = tq // tk
    hp = pl.program_id(1)
    i = pl.program_id(2)
    lane = lax.broadcasted_iota(jnp.int32, (rs, LANES), 1)
    t_i = lax.broadcasted_iota(jnp.int32, (rs, tk), 0)
    s_i = lax.broadcasted_iota(jnp.int32, (rs, tk), 1)
    u2 = u2_ref[...]
    sign_bit = jnp.uint32(0x80000000)
    chains = [(hh, sl) for sl in range(SB_SLABS) for hh in range(2)]
    q_of = []
    for hh, sl in chains:
        q_rows = q_ref[0, sl * rs:(sl + 1) * rs, :]
        q_of.append(jnp.where((lane < SB_HD) if hh == 0 else (lane >= SB_HD), q_rows, jnp.zeros_like(q_rows)))
    biases = (bias_ref[2 * hp], bias_ref[2 * hp + 1])

    def block(j, carry, key_off):
        start = pl.multiple_of(j * tk, tk)
        kblk = k_ref[0, pl.ds(start, tk), :]
        vblk = v_ref[0, pl.ds(start, tk), :]
        masks, live = {}, []
        for idx, (hh, sl) in enumerate(chains):
            if key_off is not None and key_off + 1 >= (sl + 1) * rs:
                continue
            live.append(idx)
            if key_off is not None and key_off + tk > sl * rs:
                masks[idx] = s_i + key_off < t_i + sl * rs
        zs = {idx: _nt(q_of[idx], kblk) for idx in live}
        log_w0s, log_keeps, splits = {}, {}, {}
        for idx in live:
            z = zs[idx] + biases[chains[idx][0]]
            neg_abs = lax.bitcast_convert_type(lax.bitcast_convert_type(z, jnp.uint32) | sign_bit, F32)
            s = jnp.log(1.0 + jnp.exp(neg_abs))
            log_w0 = jnp.minimum(z, 0.0) - s
            log_keep = log_w0 - z
            if idx in masks:
                log_keep = jnp.where(masks[idx], log_keep, 0.0)
            hi, lo = _split_bf16(log_keep)
            log_w0s[idx], log_keeps[idx] = log_w0, log_keep
            splits[idx] = jnp.concatenate([hi, lo], axis=1)
        sums = {idx: _nn(splits[idx], u2) for idx in live}
        ws = {}
        for idx in live:
            w = jnp.exp(log_w0s[idx] + (sums[idx] + carry[idx][0]))
            if idx in masks:
                w = jnp.where(masks[idx], w, 0.0)
            ws[idx] = w.astype(BF16)
        pv = {idx: _nn(ws[idx], vblk) for idx in live}
        return tuple((carry[idx][0] + (sums[idx][:, 0:1] + log_keeps[idx][:, 0:1]), carry[idx][1] + pv[idx])
                     if idx in pv else carry[idx] for idx in range(len(chains)))

    carry = tuple((jnp.zeros((rs, 1), F32), jnp.zeros((rs, LANES), F32)) for _ in chains)
    for d in range(n_diag):
        carry = block(i * n_diag + (n_diag - 1 - d), carry, (n_diag - 1 - d) * tk)
    n_full = i * n_diag
    carry = lax.fori_loop(0, n_full, lambda it, cr: block(n_full - 1 - it, cr, None), carry)
    for sl in range(SB_SLABS):
        o_ref[0, sl * rs:(sl + 1) * rs, :] = jnp.where(lane < SB_HD, carry[2 * sl][1],
                                                       carry[2 * sl + 1][1]).astype(o_ref.dtype)


def _p2_call(bias, qb, kb, vb, u):
    bsz, seq, _ = qb.shape
    n_hp = MIX_W // LANES
    return pl.pallas_call(
        _p2_kernel,
        grid=(bsz, n_hp, seq // TQ),
        in_specs=[pl.BlockSpec(memory_space=pltpu.SMEM),
                  pl.BlockSpec((1, TQ, LANES), lambda b, h, i: (b, i, h)),
                  pl.BlockSpec((1, seq, LANES), lambda b, h, i: (b, 0, h)),
                  pl.BlockSpec((1, seq, LANES), lambda b, h, i: (b, 0, h)),
                  pl.BlockSpec(u.shape, lambda b, h, i: (0, 0))],
        out_specs=pl.BlockSpec((1, TQ, LANES), lambda b, h, i: (b, i, h)),
        out_shape=jax.ShapeDtypeStruct((bsz, seq, MIX_W), BF16),
        compiler_params=pltpu.CompilerParams(dimension_semantics=("arbitrary", "arbitrary", "arbitrary"),
                                             vmem_limit_bytes=VMEM_LIMIT),
        name="prompt_stick_breaking",
    )(bias, qb, kb, vb, u)


def _channel_mixer_tail(x1, f, pb, wple_ref, wpg_ref, ln2g_ref, ln2b_ref):
    x1b = x1.astype(BF16)
    e = _nn(pb, wple_ref[...]) * jax.nn.sigmoid(_nn(x1b, wpg_ref[...]))
    return _ln(DN_ALPHA * x1 + f + e, ln2g_ref[...], ln2b_ref[...])


def _p3_kernel(x_ref, mac_ref, gb_ref, yb_ref, p_ref, wbrb_ref, wout_ref, wup_ref, wdown_ref, wple_ref, wpg_ref,
               ln1g_ref, ln1b_ref, cw_ref, cb_ref, ln2g_ref, ln2b_ref,
               x2_ref, conv_ref, hs_scr):
    tm = x_ref.shape[1]
    pad = SUBLANES

    @pl.when(pl.program_id(1) == 0)
    def _():
        hs_scr[0:pad, :] = jnp.zeros((pad, D2), F32)

    merged = mac_ref[0] + gb_ref[0] * _nn(yb_ref[0], wbrb_ref[...])
    h = _nn(merged.astype(BF16), wout_ref[...])
    x1 = _ln(DN_ALPHA * x_ref[0] + h, ln1g_ref[...], ln1b_ref[...])
    x1b = x1.astype(BF16)

    f = jnp.zeros((tm, D_MODEL), F32)
    for c0 in range(0, D_FF, FF_CH):
        halves = []
        for base in (c0, D_FF + c0):
            hs_scr[pad:pad + tm, base:base + FF_CH] = _nn(x1b, wup_ref[:, base:base + FF_CH])
            hc = cb_ref[:, base:base + FF_CH]
            for i in range(CONV_W):
                hc = hc + cw_ref[i:i + 1, base:base + FF_CH] * hs_scr[pad - 2 + i:pad - 2 + i + tm, base:base + FF_CH]
            halves.append(hc)
        act = (jax.nn.gelu(halves[0]) * halves[1]).astype(BF16)
        f = f + _nn(act, wdown_ref[c0:c0 + FF_CH, :])
    last = hs_scr[pad + tm - 2:pad + tm, :]
    hs_scr[pad - 2:pad, :] = last
    conv_ref[0] = last

    x2_ref[0] = _channel_mixer_tail(x1, f, p_ref[0].astype(BF16), wple_ref, wpg_ref, ln2g_ref, ln2b_ref)


def _p3_call(layer, x, mac, gb, yb, p_all, wbrb, wout, wup, wdown, wple, wpg, ln1g, ln1b, cw, cb, ln2g, ln2b):
    bsz, seq, _ = x.shape
    tm = TM3
    tile = lambda w: pl.BlockSpec((1, tm, w), lambda b, s: (b, s, 0))
    consts = (wbrb, wout, wup, wdown, wple, wpg, ln1g, ln1b, cw, cb, ln2g, ln2b)
    return pl.pallas_call(
        _p3_kernel,
        grid=(bsz, seq // tm),
        in_specs=[tile(D_MODEL), tile(D_MODEL), tile(D_MODEL), tile(MIX_W),
                  pl.BlockSpec((None, 1, tm, PLE_DIM), lambda b, s: (layer, b, s, 0))]
                 + [_const_spec(c.shape) for c in consts],
        out_specs=(tile(D_MODEL), pl.BlockSpec((1, CONV_W - 1, D2), lambda b, s: (b, 0, 0))),
        out_shape=(jax.ShapeDtypeStruct((bsz, seq, D_MODEL), F32),
                   jax.ShapeDtypeStruct((bsz, CONV_W - 1, D2), F32)),
        scratch_shapes=[pltpu.VMEM((tm + SUBLANES, D2), F32)],
        compiler_params=pltpu.CompilerParams(dimension_semantics=("arbitrary", "arbitrary"),
                                             vmem_limit_bytes=VMEM_LIMIT),
        name="prompt_channel_mixer",
    )(x, mac, gb, yb, p_all, *consts)


def _d1_kernel(x_ref, win_ref, bin_ref, ws0_ref, bs0_ref, gmg_ref, gmb_ref, wa2_ref, ba_ref, gng_ref, gnb_ref,
               wbra_ref, wbrc_ref, st_ref,
               kf_ref, vf_ref, qf_ref, gmv_ref, mac_ref, gb_ref, stn_ref, oc_scr):
    nb = x_ref.shape[0]
    xb = x_ref[...].astype(BF16)

    def proj(c0, c1):
        return _nn(xb, win_ref[:, c0:c1]) + bin_ref[:, c0:c1]

    qf_ref[...] = proj(C_Q, C_K) * (SB_HD ** -0.5)
    kf_ref[...] = proj(C_K, C_VV)
    vf_ref[...] = proj(C_VV, C_GQ)

    u = jax.nn.gelu(proj(C_U, C_V))
    vn = _ln(jax.nn.gelu(proj(C_V, C_Q)), gmg_ref[...], gmb_ref[...])
    gmv_ref[...] = vn
    y_a = u * (ws0_ref[...] * vn + bs0_ref[...])

    gq = proj(C_GQ, C_GK) * (GLA_DK ** -0.5)
    gk = proj(C_GK, C_GV)
    gv = proj(C_GV, C_GR)
    ga = proj(C_GA, C_GATE).astype(BF16)
    a = jnp.exp(_log_sigmoid(_nn(ga, wa2_ref[...]) + ba_ref[...]) * (1.0 / GLA_TAU))

    def cols(m):
        return jnp.concatenate([m, jnp.zeros((LANES - nb, GLA_KW), F32)], axis=0).T

    a_t, k_t, q_t = cols(a), cols(gk), cols(gq)
    for b in range(nb):
        s_old = st_ref[b]
        v_exp = jnp.concatenate(
            [jnp.broadcast_to(gv[b:b + 1, h * GLA_DV:(h + 1) * GLA_DV], (GLA_DK, GLA_DV)) for h in range(GLA_HEADS)],
            axis=0)
        s_new = a_t[:, b:b + 1] * s_old + k_t[:, b:b + 1] * v_exp
        stn_ref[b] = s_new
        oq = q_t[:, b:b + 1] * s_new
        for h in range(GLA_HEADS):
            oc_scr[b:b + 1, h * GLA_DV:(h + 1) * GLA_DV] = jnp.sum(oq[h * GLA_DK:(h + 1) * GLA_DK], axis=0,
                                                                     keepdims=True)
    o_c = oc_scr[...]
    o_n = jnp.concatenate(
        [_ln(o_c[:, h * GLA_DV:(h + 1) * GLA_DV], gng_ref[:, h * GLA_DV:(h + 1) * GLA_DV],
             gnb_ref[:, h * GLA_DV:(h + 1) * GLA_DV]) for h in range(GLA_HEADS)], axis=1)
    y_c = jax.nn.silu(proj(C_GR, C_GA)) * o_n

    g_a = jax.nn.sigmoid(proj(C_GATE, C_GATE + D_MODEL))
    g_c = jax.nn.sigmoid(proj(C_GATE + 2 * D_MODEL, C_GATE + 3 * D_MODEL))
    mac_ref[...] = g_a * _nn(y_a.astype(BF16), wbra_ref[...]) + g_c * _nn(y_c.astype(BF16), wbrc_ref[...])
    gb_ref[...] = jax.nn.sigmoid(proj(C_GATE + D_MODEL, C_GATE + 2 * D_MODEL))


def _d1_call(x, win, bin_, ws0, bs0, gmg, gmb, wa2, ba, gng, gnb, wbra, wbrc, st):
    nb = x.shape[0]
    out_shape = (
        jax.ShapeDtypeStruct((nb, MIX_W), F32), jax.ShapeDtypeStruct((nb, MIX_W), F32),
        jax.ShapeDtypeStruct((nb, MIX_W), F32), jax.ShapeDtypeStruct((nb, MIX_W), F32),
        jax.ShapeDtypeStruct((nb, D_MODEL), F32), jax.ShapeDtypeStruct((nb, D_MODEL), F32),
        jax.ShapeDtypeStruct((nb, GLA_KW, GLA_DV), F32),
    )
    return pl.pallas_call(
        _d1_kernel,
        out_shape=out_shape,
        scratch_shapes=[pltpu.VMEM((nb, MIX_W), F32)],
        compiler_params=pltpu.CompilerParams(vmem_limit_bytes=VMEM_LIMIT),
        name="sample_proj_mixers",
    )(x, win, bin_, ws0, bs0, gmg, gmb, wa2, ba, gng, gnb, wbra, wbrc, st)


def _d2_kernel(pt_ref, q_ref, bias_ref, uo_ref, glater_ref, gall_ref, *refs):
    n = PAGES_PER_STEP
    k_refs, v_refs = refs[:n], refs[n:2 * n]
    o_ref, acc_scr, c_scr = refs[2 * n], refs[2 * n + 1], refs[2 * n + 2]
    j = pl.program_id(1)

    @pl.when(j == 0)
    def _():
        acc_scr[...] = jnp.zeros_like(acc_scr)
        c_scr[...] = jnp.zeros_like(c_scr)

    row_head = lax.broadcasted_iota(jnp.int32, (SB_HEADS, MIX_W), 0)
    col_head = lax.broadcasted_iota(jnp.int32, (SB_HEADS, MIX_W), 1) // SB_HD
    qbd = jnp.where(row_head == col_head, jnp.broadcast_to(q_ref[0], (SB_HEADS, MIX_W)), 0.0).astype(BF16)
    z = jnp.concatenate([_nn(qbd, k_refs[i][0].astype(BF16)) for i in range(n)], axis=0) + bias_ref[...]
    sp = _softplus(z)
    log_keep = -sp
    hi, lo = _split_bf16(log_keep)
    r2 = _nn(hi, uo_ref[...]) + _nn(lo, uo_ref[...])
    r, tot = r2[:, :PAGE], r2[:, PAGE:]
    t_hi, t_lo = _split_bf16(tot)
    later = _nn(glater_ref[...], t_hi) + _nn(glater_ref[...], t_lo)
    c = c_scr[...]
    w = jnp.exp((z - sp) + (r + (c + later)))
    c_scr[...] = c + _nn(gall_ref[...], t_hi) + _nn(gall_ref[...], t_lo)
    for h in range(SB_HEADS):
        rows = slice(h * SB_HD, (h + 1) * SB_HD)
        acc_h = acc_scr[rows, :]
        for i in range(n):
            acc_h = acc_h + v_refs[i][0, rows, :] * w[i * SB_HEADS + h:i * SB_HEADS + h + 1, :]
        acc_scr[rows, :] = acc_h

    @pl.when(j == pl.num_programs(1) - 1)
    def _():
        o_ref[0] = jnp.sum(acc_scr[...], axis=1, keepdims=True)


def _d2_call(layer, page_table, q, bias_rows, uo, glater, gall, cache_k_t, cache_v_t):
    nb, n_pages = page_table.shape
    n = PAGES_PER_STEP
    steps = n_pages // n

    def page_spec(i):
        return pl.BlockSpec((None, 1, MIX_W, PAGE),
                            lambda b, j, pt: (layer, pt[b, n_pages - 1 - (j * n + i)], 0, 0))

    const = lambda shape: pl.BlockSpec(shape, lambda b, j, pt: (0,) * len(shape))
    grid_spec = pltpu.PrefetchScalarGridSpec(
        num_scalar_prefetch=1,
        grid=(nb, steps),
        in_specs=[pl.BlockSpec((1, 1, MIX_W), lambda b, j, pt: (b, 0, 0)),
                  const(bias_rows.shape), const(uo.shape), const(glater.shape), const(gall.shape)]
                 + [page_spec(i) for i in range(n)] + [page_spec(i) for i in range(n)],
        out_specs=pl.BlockSpec((1, MIX_W, 1), lambda b, j, pt: (b, 0, 0)),
        scratch_shapes=[pltpu.VMEM((MIX_W, PAGE), F32), pltpu.VMEM((n * SB_HEADS, PAGE), F32)],
    )
    return pl.pallas_call(
        _d2_kernel,
        grid_spec=grid_spec,
        out_shape=jax.ShapeDtypeStruct((nb, MIX_W, 1), F32),
        compiler_params=pltpu.CompilerParams(dimension_semantics=("arbitrary", "arbitrary"),
                                             vmem_limit_bytes=VMEM_LIMIT),
        name="sample_paged_stick_breaking",
    )(page_table, q, bias_rows, uo, glater, gall, *([cache_k_t] * n), *([cache_v_t] * n))


def _d3_kernel(x_ref, mac_ref, gb_ref, yb_ref, p_ref, prev0_ref, prev1_ref, wbrb_ref, wout_ref, wup_ref, wdown_ref,
               wple_ref, wpg_ref, ln1g_ref, ln1b_ref, cw_ref, cb_ref, ln2g_ref, ln2b_ref,
               x2_ref, hup_ref):
    nb = x_ref.shape[0]
    merged = mac_ref[...] + gb_ref[...] * _nn(yb_ref[...].astype(BF16), wbrb_ref[...])
    h = _nn(merged.astype(BF16), wout_ref[...])
    x1 = _ln(DN_ALPHA * x_ref[...] + h, ln1g_ref[...], ln1b_ref[...])
    x1b = x1.astype(BF16)
    f = jnp.zeros((nb, D_MODEL), F32)
    for c0 in range(0, D_FF, FF_CH):
        halves = []
        for base in (c0, D_FF + c0):
            sl = slice(base, base + FF_CH)
            hup = _nn(x1b, wup_ref[:, sl])
            hup_ref[:, sl] = hup
            halves.append(cb_ref[:, sl] + cw_ref[0:1, sl] * prev0_ref[:, sl] + cw_ref[1:2, sl] * prev1_ref[:, sl]
                          + cw_ref[2:3, sl] * hup)
        act = (jax.nn.gelu(halves[0]) * halves[1]).astype(BF16)
        f = f + _nn(act, wdown_ref[c0:c0 + FF_CH, :])
    x2_ref[...] = _channel_mixer_tail(x1, f, p_ref[...].astype(BF16), wple_ref, wpg_ref, ln2g_ref, ln2b_ref)


def _d3_call(x, mac, gb, yb, p, prev0, prev1, wbrb, wout, wup, wdown, wple, wpg, ln1g, ln1b, cw, cb, ln2g, ln2b):
    nb = x.shape[0]
    return pl.pallas_call(
        _d3_kernel,
        out_shape=(jax.ShapeDtypeStruct((nb, D_MODEL), F32), jax.ShapeDtypeStruct((nb, D2), F32)),
        compiler_params=pltpu.CompilerParams(vmem_limit_bytes=VMEM_LIMIT),
        name="sample_channel_mixer",
    )(x, mac, gb, yb, p, prev0, prev1, wbrb, wout, wup, wdown, wple, wpg, ln1g, ln1b, cw, cb, ln2g, ln2b)


def _constants():
    tri = np.zeros((TM1, TM1), np.float32)
    for c in range(TM1 // GLA_CHUNK):
        r0 = c * GLA_CHUNK
        tri[r0:r0 + GLA_CHUNK, r0:r0 + GLA_CHUNK] = np.tril(np.ones((GLA_CHUNK, GLA_CHUNK), np.float32))
    jj = np.arange(TK)
    u_sb = np.tile((jj[:, None] > jj[None, :]).astype(np.float32), (2, 1))
    kk = np.arange(PAGE)
    uo = np.concatenate([(kk[:, None] > kk[None, :]).astype(np.float32), np.ones((PAGE, PAGE), np.float32)], axis=1)
    rows = np.arange(PAGES_PER_STEP * SB_HEADS)
    page_of, head_of = rows // SB_HEADS, rows % SB_HEADS
    same_head = head_of[:, None] == head_of[None, :]
    glater = (same_head & (page_of[None, :] < page_of[:, None])).astype(np.float32)
    gall = same_head.astype(np.float32)
    as_bf16 = lambda m: jnp.asarray(m, BF16)
    return as_bf16(tri), as_bf16(u_sb), as_bf16(uo), as_bf16(glater), as_bf16(gall)


def _row(vec):
    return vec.reshape(1, -1)


def kernel(x_prompt, x_sample, cache_k, cache_v, state_gla, state_conv, page_table, p_prompt, p_sample, w_in, b_in, gm_ws, gm_bs, gm_ln_g, gm_ln_b, sb_bias, gla_wa2, gla_ba, gla_gn_g, gla_gn_b, w_br_a, w_br_b, w_br_c, w_out, ln1_g, ln1_b, w_up, conv_w, conv_b, w_down, w_ple, w_pgate, ln2_g, ln2_b):
    bp, sp = x_prompt.shape[0], x_prompt.shape[1]
    bd = x_sample.shape[0]
    n_pool = cache_k.shape[1]
    tri, u_sb, uo, glater, gall = _constants()
    ck = jnp.transpose(cache_k, (0, 1, 3, 4, 2)).reshape(DEPTH, n_pool, MIX_W, PAGE)
    cv = jnp.transpose(cache_v, (0, 1, 3, 4, 2)).reshape(DEPTH, n_pool, MIX_W, PAGE)

    yp = x_prompt
    yd = x_sample.reshape(bd, D_MODEL)
    outs = [[] for _ in range(9)]
    for i in range(DEPTH):
        ga_pad = jnp.zeros((D_MODEL, LANES - GLA_RANK), F32)
        win = jnp.concatenate([w_in[i][:, :C_GA], w_in[i][:, C_GA:C_GA + GLA_RANK], ga_pad,
                               w_in[i][:, C_GA + GLA_RANK:]], axis=1).astype(BF16)
        bin_ = jnp.concatenate([b_in[i][:C_GA + GLA_RANK], jnp.zeros((LANES - GLA_RANK,), F32),
                                b_in[i][C_GA + GLA_RANK:]]).reshape(1, D_INP)
        wa2 = jnp.concatenate([gla_wa2[i], jnp.zeros((LANES - GLA_RANK, GLA_KW), F32)], axis=0).astype(BF16)
        bsf = jnp.repeat(gm_bs[i].T, LANES, axis=1)
        ws0 = _row(jnp.repeat(gm_ws[i][:, 0, 0], LANES))
        bs0 = _row(jnp.repeat(gm_bs[i][:, 0], LANES))
        gmg, gmb, ba = _row(gm_ln_g[i]), _row(gm_ln_b[i]), _row(gla_ba[i])
        gng, gnb = _row(gla_gn_g[i]), _row(gla_gn_b[i])
        wbra, wbrb, wbrc = w_br_a[i].astype(BF16), w_br_b[i].astype(BF16), w_br_c[i].astype(BF16)
        wout, wup, wdown = w_out[i].astype(BF16), w_up[i].astype(BF16), w_down[i].astype(BF16)
        wple, wpg = w_ple[i].astype(BF16), w_pgate[i].astype(BF16)
        ln1g, ln1b, ln2g, ln2b = _row(ln1_g[i]), _row(ln1_b[i]), _row(ln2_g[i]), _row(ln2_b[i])
        cw, cb = conv_w[i], _row(conv_b[i])

        kf, vf, qb, kb, vb, mac, gb, s2 = _p1_call(yp, win, bin_, gm_ws[i], bsf, gmg, gmb, wa2, ba, gng, gnb,
                                                    wbra, wbrc, tri)
        yb = _p2_call(sb_bias[i], qb, kb, vb, u_sb)
        yp, conv_p = _p3_call(i, yp, mac, gb, yb, p_prompt, wbrb, wout, wup, wdown, wple, wpg,
                              ln1g, ln1b, cw, cb, ln2g, ln2b)
        gla_p = s2.reshape(bp, GLA_DV, GLA_HEADS, GLA_DK).transpose(0, 2, 3, 1)

        st = state_gla[i].reshape(bd, GLA_KW, GLA_DV)
        kd, vd, qd, gmv, mac_d, gb_d, st_new = _d1_call(yd, win, bin_, ws0, bs0, gmg, gmb, wa2, ba, gng, gnb,
                                                         wbra, wbrc, st)
        bias_rows = jnp.broadcast_to(jnp.tile(sb_bias[i], PAGES_PER_STEP)[:, None],
                                     (PAGES_PER_STEP * SB_HEADS, PAGE))
        yb_d = _d2_call(i, page_table, qd.reshape(bd, 1, MIX_W), bias_rows, uo, glater, gall, ck, cv)
        yb_d = yb_d.reshape(bd, MIX_W)
        yd, hup_d = _d3_call(yd, mac_d, gb_d, yb_d, p_sample[i].reshape(bd, PLE_DIM),
                             state_conv[i][:, 0], state_conv[i][:, 1],
                             wbrb, wout, wup, wdown, wple, wpg, ln1g, ln1b, cw, cb, ln2g, ln2b)

        outs[0].append(kf.reshape(bp, sp, SB_HEADS, SB_HD))
        outs[1].append(vf.reshape(bp, sp, SB_HEADS, SB_HD))
        outs[2].append(kd.reshape(bd, 1, SB_HEADS, SB_HD))
        outs[3].append(vd.reshape(bd, 1, SB_HEADS, SB_HD))
        outs[4].append(gla_p)
        outs[5].append(st_new.reshape(bd, GLA_HEADS, GLA_DK, GLA_DV))
        outs[6].append(conv_p)
        outs[7].append(jnp.stack([state_conv[i][:, 1], hup_d], axis=1))
        outs[8].append(gmv.reshape(bd, 1, MIX_W))

    return (yp, yd.reshape(bd, 1, D_MODEL)) + tuple(jnp.stack(o) for o in outs)
```

```python
import functools

import numpy as np
import jax
import jax.numpy as jnp
from jax import lax
from jax.experimental import pallas as pl
from jax.experimental.pallas import tpu as pltpu

F32 = jnp.float32
BF16 = jnp.bfloat16

D_MODEL = 1024
DEPTH = 2
PAGE = 128
MIX_W = 512
GM_CHUNK = 128
GM_GROUPS = 4
SB_HEADS = 8
SB_HD = 64
GLA_HEADS = 4
GLA_DK = 64
GLA_DV = 128
GLA_KW = 256
GLA_RANK = 16
GLA_TAU = 16.0
GLA_CHUNK = 64
D_FF = 2816
D2 = 2 * D_FF
CONV_W = 3
PLE_DIM = 256
LN_EPS = 1e-5
DN_ALPHA = (2.0 * DEPTH) ** 0.25
LANES = 128
SUBLANES = 8

C_U, C_V, C_Q, C_K, C_VV = 0, 512, 1024, 1536, 2048
C_GQ, C_GK, C_GV, C_GR, C_GA, C_GATE = 2560, 2816, 3072, 3584, 4096, 4224
D_INP = C_GATE + 3 * D_MODEL

TM1 = 256
TM3 = 256
TQ = 512
TK = 256
SB_SLABS = 2
SB_KV_PER_BODY = 2
assert TQ % TK == 0 and (TQ // TK) % SB_KV_PER_BODY == 0
FF_CH = 256
PAGES_PER_STEP = 16
VMEM_LIMIT = 56 * 1024 * 1024


def _nt(a, b):
    return lax.dot_general(a, b, (((1,), (1,)), ((), ())), preferred_element_type=F32)


def _tn(a, b):
    return lax.dot_general(a, b, (((0,), (0,)), ((), ())), preferred_element_type=F32)


def _nn(a, b):
    return jnp.dot(a, b, preferred_element_type=F32)


def _ln(x, g, b):
    mu = jnp.mean(x, axis=-1, keepdims=True)
    xc = x - mu
    var = jnp.mean(xc * xc, axis=-1, keepdims=True)
    return xc * lax.rsqrt(var + LN_EPS) * g + b


def _softplus(z):
    return jnp.maximum(z, 0.0) + jnp.log1p(jnp.exp(-jnp.abs(z)))


def _log_sigmoid(z):
    return jnp.minimum(z, 0.0) - jnp.log1p(jnp.exp(-jnp.abs(z)))


def _split_bf16(x):
    hi = x.astype(BF16)
    lo = (x - hi.astype(F32)).astype(BF16)
    return hi, lo


def _const_spec(shape):
    nd = len(shape)
    return pl.BlockSpec(shape, lambda *_: (0,) * nd, pipeline_mode=pl.Buffered(1))


def _p1_kernel(x_ref, win_ref, bin_ref, ws_ref, bsf_ref, gmg_ref, gmb_ref, wa2_ref, ba_ref, gng_ref, gnb_ref,
               wbra_ref, wbrc_ref, tri_ref,
               kf_ref, vf_ref, qb_ref, kb_ref, vb_ref, mac_ref, gb_ref, st_ref, s2_scr):
    tm = x_ref.shape[1]

    @pl.when(pl.program_id(1) == 0)
    def _():
        s2_scr[...] = jnp.zeros_like(s2_scr)

    xb = x_ref[0].astype(BF16)

    def proj(c0, c1):
        return _nn(xb, win_ref[:, c0:c1]) + bin_ref[:, c0:c1]

    q = proj(C_Q, C_K)
    k = proj(C_K, C_VV)
    v = proj(C_VV, C_GQ)
    kf_ref[0] = k
    vf_ref[0] = v
    qb_ref[0] = (q * (SB_HD ** -0.5)).astype(BF16)
    kb_ref[0] = k.astype(BF16)
    vb_ref[0] = v.astype(BF16)

    u = jax.nn.gelu(proj(C_U, C_V))
    vn = _ln(jax.nn.gelu(proj(C_V, C_Q)), gmg_ref[...], gmb_ref[...])
    vnb = vn.astype(BF16)
    r_i = lax.broadcasted_iota(jnp.int32, (GM_CHUNK, GM_CHUNK), 0)
    c_i = lax.broadcasted_iota(jnp.int32, (GM_CHUNK, GM_CHUNK), 1)
    ws_m = [jnp.where(r_i >= c_i, ws_ref[g], 0.0).astype(BF16) for g in range(GM_GROUPS)]
    sv_rows = []
    for n in range(tm // GM_CHUNK):
        cols = [_nn(ws_m[g], vnb[n * GM_CHUNK:(n + 1) * GM_CHUNK, g * LANES:(g + 1) * LANES])
                for g in range(GM_GROUPS)]
        sv_rows.append(jnp.concatenate(cols, axis=1) + bsf_ref[...])
    y_a = u * jnp.concatenate(sv_rows, axis=0)

    gq = proj(C_GQ, C_GK) * (GLA_DK ** -0.5)
    gk = proj(C_GK, C_GV)
    gv = proj(C_GV, C_GR).astype(BF16)
    ga = proj(C_GA, C_GATE).astype(BF16)
    la = _log_sigmoid(_nn(ga, wa2_ref[...]) + ba_ref[...]) * (1.0 / GLA_TAU)
    la_hi, la_lo = _split_bf16(la)
    tri = tri_ref[...]
    bcum = _nn(tri, la_hi) + _nn(tri, la_lo)
    head_of_lane = lax.broadcasted_iota(jnp.int32, (GLA_CHUNK, GLA_KW), 1) // GLA_DK
    t_i = lax.broadcasted_iota(jnp.int32, (GLA_CHUNK, GLA_CHUNK), 0)
    s_i = lax.broadcasted_iota(jnp.int32, (GLA_CHUNK, GLA_CHUNK), 1)
    causal = s_i <= t_i
    n_chunks = tm // GLA_CHUNK
    heads = range(GLA_HEADS)
    masks = [head_of_lane == h for h in heads]
    qe_m, v_h, decay, att, upd = [], [], [], [], []
    for c in range(n_chunks):
        r0 = c * GLA_CHUNK
        bc = bcum[r0:r0 + GLA_CHUNK]
        bref = bc[GLA_CHUNK // 2:GLA_CHUNK // 2 + 1]
        blast = bc[GLA_CHUNK - 1:GLA_CHUNK]
        qc = gq[r0:r0 + GLA_CHUNK]
        kc = gk[r0:r0 + GLA_CHUNK]
        vc = gv[r0:r0 + GLA_CHUNK]
        qe = qc * jnp.exp(bc)
        qt = qc * jnp.exp(bc - bref)
        kt = (kc * jnp.exp(bref - bc)).astype(BF16)
        kd = kc * jnp.exp(blast - bc)
        qe_m.append([jnp.where(masks[h], qe, 0.0).astype(BF16) for h in heads])
        v_h.append([vc[:, h * GLA_DV:(h + 1) * GLA_DV] for h in heads])
        decay.append(jnp.exp(blast))
        att.append([_nt(jnp.where(masks[h], qt, 0.0).astype(BF16), kt) for h in heads])
        u_c = _tn(v_h[c][0], jnp.where(masks[0], kd, 0.0).astype(BF16))
        for h in range(1, GLA_HEADS):
            u_c = u_c + _tn(v_h[c][h], jnp.where(masks[h], kd, 0.0).astype(BF16))
        upd.append(u_c)
    o_intra = [[_nn(jnp.where(causal, att[c][h], 0.0).astype(BF16), v_h[c][h]) for h in heads]
               for c in range(n_chunks)]
    state = s2_scr[...]
    states = []
    for c in range(n_chunks):
        states.append(state.astype(BF16))
        state = state * decay[c] + upd[c]
    s2_scr[...] = state
    st_ref[0] = state
    o_c = jnp.concatenate(
        [jnp.concatenate([o_intra[c][h] + _nt(qe_m[c][h], states[c]) for h in heads], axis=1)
         for c in range(n_chunks)], axis=0)
    o_n = jnp.concatenate(
        [_ln(o_c[:, h * GLA_DV:(h + 1) * GLA_DV], gng_ref[:, h * GLA_DV:(h + 1) * GLA_DV],
             gnb_ref[:, h * GLA_DV:(h + 1) * GLA_DV]) for h in range(GLA_HEADS)], axis=1)
    y_c = jax.nn.silu(proj(C_GR, C_GA)) * o_n

    g_a = jax.nn.sigmoid(proj(C_GATE, C_GATE + D_MODEL))
    g_c = jax.nn.sigmoid(proj(C_GATE + 2 * D_MODEL, C_GATE + 3 * D_MODEL))
    mac_ref[0] = g_a * _nn(y_a.astype(BF16), wbra_ref[...]) + g_c * _nn(y_c.astype(BF16), wbrc_ref[...])
    gb_ref[0] = jax.nn.sigmoid(proj(C_GATE + D_MODEL, C_GATE + 2 * D_MODEL))


def _p1_call(x, win, bin_, ws, bsf, gmg, gmb, wa2, ba, gng, gnb, wbra, wbrc, tri):
    bsz, seq, _ = x.shape
    tm = TM1
    tile = lambda w: pl.BlockSpec((1, tm, w), lambda b, s: (b, s, 0))
    consts = (win, bin_, ws, bsf, gmg, gmb, wa2, ba, gng, gnb, wbra, wbrc, tri)
    out_shape = (
        jax.ShapeDtypeStruct((bsz, seq, MIX_W), F32), jax.ShapeDtypeStruct((bsz, seq, MIX_W), F32),
        jax.ShapeDtypeStruct((bsz, seq, MIX_W), BF16), jax.ShapeDtypeStruct((bsz, seq, MIX_W), BF16),
        jax.ShapeDtypeStruct((bsz, seq, MIX_W), BF16),
        jax.ShapeDtypeStruct((bsz, seq, D_MODEL), F32), jax.ShapeDtypeStruct((bsz, seq, D_MODEL), F32),
        jax.ShapeDtypeStruct((bsz, GLA_DV, GLA_KW), F32),
    )
    out_specs = (tile(MIX_W), tile(MIX_W), tile(MIX_W), tile(MIX_W), tile(MIX_W), tile(D_MODEL), tile(D_MODEL),
                 pl.BlockSpec((1, GLA_DV, GLA_KW), lambda b, s: (b, 0, 0)))
    return pl.pallas_call(
        _p1_kernel,
        grid=(bsz, seq // tm),
        in_specs=[tile(D_MODEL)] + [_const_spec(c.shape) for c in consts],
        out_specs=out_specs,
        out_shape=out_shape,
        scratch_shapes=[pltpu.VMEM((GLA_DV, GLA_KW), F32)],
        compiler_params=pltpu.CompilerParams(dimension_semantics=("arbitrary", "arbitrary"),
                                             vmem_limit_bytes=VMEM_LIMIT),
        name="prompt_proj_mixers",
    )(x, *consts)


def _p2_kernel(bias_ref, q_ref, k_ref, v_ref, u2_ref, o_ref):
    tq = q_ref.shape[1]
    tk = TK
    rs = tq // SB_SLABS
    n_diag = tq // tk
    hp = pl.program_id(1)
    i = pl.program_id(2)
    lane = lax.broadcasted_iota(jnp.int32, (rs, LANES), 1)
    t_i = lax.broadcasted_iota(jnp.int32, (rs, tk), 0)
    s_i = lax.broadcasted_iota(jnp.int32, (rs, tk), 1)
    u2 = u2_ref[...]
    sign_bit = jnp.uint32(0x80000000)
    chains = [(hh, sl) for sl in range(SB_SLABS) for hh in range(2)]
    q_of = []
    for hh, sl in chains:
        q_rows = q_ref[0, sl * rs:(sl + 1) * rs, :]
        q_of.append(jnp.where((lane < SB_HD) if hh == 0 else (lane >= SB_HD), q_rows, jnp.zeros_like(q_rows)))
    biases = (bias_ref[2 * hp], bias_ref[2 * hp + 1])

    def blocks(js, carry, key_offs):
        kblk = [k_ref[0, pl.ds(pl.multiple_of(j * tk, tk), tk), :] for j in js]
        vblk = [v_ref[0, pl.ds(pl.multiple_of(j * tk, tk), tk), :] for j in js]
        masks, live = {}, []
        for b, key_off in enumerate(key_offs):
            for idx, (hh, sl) in enumerate(chains):
                if key_off is not None and key_off + 1 >= (sl + 1) * rs:
                    continue
                live.append((b, idx))
                if key_off is not None and key_off + tk > sl * rs:
                    masks[b, idx] = s_i + key_off < t_i + sl * rs
        zs = {e: _nt(q_of[e[1]], kblk[e[0]]) for e in live}
        log_w0s, log_keeps, splits = {}, {}, {}
        for e in live:
            z = zs[e] + biases[chains[e[1]][0]]
            neg_abs = lax.bitcast_convert_type(lax.bitcast_convert_type(z, jnp.uint32) | sign_bit, F32)
            s = jnp.log(1.0 + jnp.exp(neg_abs))
            log_w0 = jnp.minimum(z, 0.0) - s
            log_keep = log_w0 - z
            if e in masks:
                log_keep = jnp.where(masks[e], log_keep, 0.0)
            log_w0s[e], log_keeps[e] = log_w0, log_keep
            splits[e] = log_keep.astype(BF16)
        sums = {e: _nn(splits[e], u2) for e in live}
        c_run = [carry[idx][0] for idx in range(len(chains))]
        ws = {}
        for e in live:
            b, idx = e
            w = jnp.exp(log_w0s[e] + (sums[e] + c_run[idx]))
            if e in masks:
                w = jnp.where(masks[e], w, 0.0)
            ws[e] = w.astype(BF16)
            c_run[idx] = c_run[idx] + (sums[e][:, 0:1] + log_keeps[e][:, 0:1])
        acc = [carry[idx][1] for idx in range(len(chains))]
        for e in live:
            acc[e[1]] = acc[e[1]] + _nn(ws[e], vblk[e[0]])
        return tuple((c_run[idx], acc[idx]) for idx in range(len(chains)))

    carry = tuple((jnp.zeros((rs, 1), F32), jnp.zeros((rs, LANES), F32)) for _ in chains)
    carry = blocks([i * n_diag + (n_diag - 1 - d) for d in range(n_diag)], carry,
                   [(n_diag - 1 - d) * tk for d in range(n_diag)])
    n_groups = (i * n_diag) // SB_KV_PER_BODY

    def body(it, cr):
        top = (n_groups - it) * SB_KV_PER_BODY - 1
        return blocks([top - b for b in range(SB_KV_PER_BODY)], cr, [None] * SB_KV_PER_BODY)

    carry = lax.fori_loop(0, n_groups, body, carry)
    for sl in range(SB_SLABS):
        o_ref[0, sl * rs:(sl + 1) * rs, :] = jnp.where(lane < SB_HD, carry[2 * sl][1],
                                                       carry[2 * sl + 1][1]).astype(o_ref.dtype)


def _p2_call(bias, qb, kb, vb, u):
    bsz, seq, _ = qb.shape
    n_hp = MIX_W // LANES
    return pl.pallas_call(
        _p2_kernel,
        grid=(bsz, n_hp, seq // TQ),
        in_specs=[pl.BlockSpec(memory_space=pltpu.SMEM),
                  pl.BlockSpec((1, TQ, LANES), lambda b, h, i: (b, i, h)),
                  pl.BlockSpec((1, seq, LANES), lambda b, h, i: (b, 0, h)),
                  pl.BlockSpec((1, seq, LANES), lambda b, h, i: (b, 0, h)),
                  pl.BlockSpec(u.shape, lambda b, h, i: (0, 0))],
        out_specs=pl.BlockSpec((1, TQ, LANES), lambda b, h, i: (b, i, h)),
        out_shape=jax.ShapeDtypeStruct((bsz, seq, MIX_W), BF16),
        compiler_params=pltpu.CompilerParams(dimension_semantics=("arbitrary", "arbitrary", "arbitrary"),
                                             vmem_limit_bytes=VMEM_LIMIT),
        name="prompt_stick_breaking",
    )(bias, qb, kb, vb, u)


def _channel_mixer_tail(x1, f, pb, wple_ref, wpg_ref, ln2g_ref, ln2b_ref):
    x1b = x1.astype(BF16)
    e = _nn(pb, wple_ref[...]) * jax.nn.sigmoid(_nn(x1b, wpg_ref[...]))
    return _ln(DN_ALPHA * x1 + f + e, ln2g_ref[...], ln2b_ref[...])


def _p3_kernel(x_ref, mac_ref, gb_ref, yb_ref, p_ref, wbrb_ref, wout_ref, wup_ref, wdown_ref, wple_ref, wpg_ref,
               ln1g_ref, ln1b_ref, cw_ref, cb_ref, ln2g_ref, ln2b_ref,
               x2_ref, conv_ref, hs_scr):
    tm = x_ref.shape[1]
    pad = SUBLANES

    @pl.when(pl.program_id(1) == 0)
    def _():
        hs_scr[0:pad, :] = jnp.zeros((pad, D2), F32)

    merged = mac_ref[0] + gb_ref[0] * _nn(yb_ref[0], wbrb_ref[...])
    h = _nn(merged.astype(BF16), wout_ref[...])
    x1 = _ln(DN_ALPHA * x_ref[0] + h, ln1g_ref[...], ln1b_ref[...])
    x1b = x1.astype(BF16)

    def up(c0):
        return [_nn(x1b, wup_ref[:, base:base + FF_CH]) for base in (c0, D_FF + c0)]

    def conv_act(c0, hups):
        halves = []
        for base, hup in zip((c0, D_FF + c0), hups):
            cols = slice(base, base + FF_CH)
            hs_scr[pad:pad + tm, cols] = hup
            hc = cb_ref[:, cols] + cw_ref[CONV_W - 1:CONV_W, cols] * hup
            for i in range(CONV_W - 1):
                hc = hc + cw_ref[i:i + 1, cols] * hs_scr[pad - 2 + i:pad - 2 + i + tm, cols]
            halves.append(hc)
        return (jax.nn.gelu(halves[0]) * halves[1]).astype(BF16)

    chunks = list(range(0, D_FF, FF_CH))
    f = jnp.zeros((tm, D_MODEL), F32)
    hups = up(chunks[0])
    act_prev = None
    for n, c0 in enumerate(chunks):
        if act_prev is not None:
            f = f + _nn(act_prev, wdown_ref[chunks[n - 1]:chunks[n - 1] + FF_CH, :])
        hups_next = up(chunks[n + 1]) if n + 1 < len(chunks) else None
        act_prev = conv_act(c0, hups)
        hups = hups_next
    f = f + _nn(act_prev, wdown_ref[chunks[-1]:chunks[-1] + FF_CH, :])
    last = hs_scr[pad + tm - 2:pad + tm, :]
    hs_scr[pad - 2:pad, :] = last
    conv_ref[0] = last

    x2_ref[0] = _channel_mixer_tail(x1, f, p_ref[0].astype(BF16), wple_ref, wpg_ref, ln2g_ref, ln2b_ref)


def _p3_call(layer, x, mac, gb, yb, p_all, wbrb, wout, wup, wdown, wple, wpg, ln1g, ln1b, cw, cb, ln2g, ln2b):
    bsz, seq, _ = x.shape
    tm = TM3
    tile = lambda w: pl.BlockSpec((1, tm, w), lambda b, s: (b, s, 0))
    consts = (wbrb, wout, wup, wdown, wple, wpg, ln1g, ln1b, cw, cb, ln2g, ln2b)
    return pl.pallas_call(
        _p3_kernel,
        grid=(bsz, seq // tm),
        in_specs=[tile(D_MODEL), tile(D_MODEL), tile(D_MODEL), tile(MIX_W),
                  pl.BlockSpec((None, 1, tm, PLE_DIM), lambda b, s: (layer, b, s, 0))]
                 + [_const_spec(c.shape) for c in consts],
        out_specs=(tile(D_MODEL), pl.BlockSpec((1, CONV_W - 1, D2), lambda b, s: (b, 0, 0))),
        out_shape=(jax.ShapeDtypeStruct((bsz, seq, D_MODEL), F32),
                   jax.ShapeDtypeStruct((bsz, CONV_W - 1, D2), F32)),
        scratch_shapes=[pltpu.VMEM((tm + SUBLANES, D2), F32)],
        compiler_params=pltpu.CompilerParams(dimension_semantics=("arbitrary", "arbitrary"),
                                             vmem_limit_bytes=VMEM_LIMIT),
        name="prompt_channel_mixer",
    )(x, mac, gb, yb, p_all, *consts)


def _d1_kernel(x_ref, win_ref, bin_ref, ws0_ref, bs0_ref, gmg_ref, gmb_ref, wa2_ref, ba_ref, gng_ref, gnb_ref,
               wbra_ref, wbrc_ref, st_ref,
               kf_ref, vf_ref, qf_ref, gmv_ref, mac_ref, gb_ref, stn_ref, oc_scr):
    nb = x_ref.shape[0]
    xb = x_ref[...].astype(BF16)

    def proj(c0, c1):
        return _nn(xb, win_ref[:, c0:c1]) + bin_ref[:, c0:c1]

    qf_ref[...] = proj(C_Q, C_K) * (SB_HD ** -0.5)
    kf_ref[...] = proj(C_K, C_VV)
    vf_ref[...] = proj(C_VV, C_GQ)

    u = jax.nn.gelu(proj(C_U, C_V))
    vn = _ln(jax.nn.gelu(proj(C_V, C_Q)), gmg_ref[...], gmb_ref[...])
    gmv_ref[...] = vn
    y_a = u * (ws0_ref[...] * vn + bs0_ref[...])

    gq = proj(C_GQ, C_GK) * (GLA_DK ** -0.5)
    gk = proj(C_GK, C_GV)
    gv = proj(C_GV, C_GR)
    ga = proj(C_GA, C_GATE).astype(BF16)
    a = jnp.exp(_log_sigmoid(_nn(ga, wa2_ref[...]) + ba_ref[...]) * (1.0 / GLA_TAU))

    def cols(m):
        return jnp.concatenate([m, jnp.zeros((LANES - nb, GLA_KW), F32)], axis=0).T

    a_t, k_t, q_t = cols(a), cols(gk), cols(gq)
    for b in range(nb):
        s_old = st_ref[b]
        v_exp = jnp.concatenate(
            [jnp.broadcast_to(gv[b:b + 1, h * GLA_DV:(h + 1) * GLA_DV], (GLA_DK, GLA_DV)) for h in range(GLA_HEADS)],
            axis=0)
        s_new = a_t[:, b:b + 1] * s_old + k_t[:, b:b + 1] * v_exp
        stn_ref[b] = s_new
        oq = q_t[:, b:b + 1] * s_new
        for h in range(GLA_HEADS):
            oc_scr[b:b + 1, h * GLA_DV:(h + 1) * GLA_DV] = jnp.sum(oq[h * GLA_DK:(h + 1) * GLA_DK], axis=0,
                                                                     keepdims=True)
    o_c = oc_scr[...]
    o_n = jnp.concatenate(
        [_ln(o_c[:, h * GLA_DV:(h + 1) * GLA_DV], gng_ref[:, h * GLA_DV:(h + 1) * GLA_DV],
             gnb_ref[:, h * GLA_DV:(h + 1) * GLA_DV]) for h in range(GLA_HEADS)], axis=1)
    y_c = jax.nn.silu(proj(C_GR, C_GA)) * o_n

    g_a = jax.nn.sigmoid(proj(C_GATE, C_GATE + D_MODEL))
    g_c = jax.nn.sigmoid(proj(C_GATE + 2 * D_MODEL, C_GATE + 3 * D_MODEL))
    mac_ref[...] = g_a * _nn(y_a.astype(BF16), wbra_ref[...]) + g_c * _nn(y_c.astype(BF16), wbrc_ref[...])
    gb_ref[...] = jax.nn.sigmoid(proj(C_GATE + D_MODEL, C_GATE + 2 * D_MODEL))


def _d1_call(x, win, bin_, ws0, bs0, gmg, gmb, wa2, ba, gng, gnb, wbra, wbrc, st):
    nb = x.shape[0]
    out_shape = (
        jax.ShapeDtypeStruct((nb, MIX_W), F32), jax.ShapeDtypeStruct((nb, MIX_W), F32),
        jax.ShapeDtypeStruct((nb, MIX_W), F32), jax.ShapeDtypeStruct((nb, MIX_W), F32),
        jax.ShapeDtypeStruct((nb, D_MODEL), F32), jax.ShapeDtypeStruct((nb, D_MODEL), F32),
        jax.ShapeDtypeStruct((nb, GLA_KW, GLA_DV), F32),
    )
    return pl.pallas_call(
        _d1_kernel,
        out_shape=out_shape,
        scratch_shapes=[pltpu.VMEM((nb, MIX_W), F32)],
        compiler_params=pltpu.CompilerParams(vmem_limit_bytes=VMEM_LIMIT),
        name="sample_proj_mixers",
    )(x, win, bin_, ws0, bs0, gmg, gmb, wa2, ba, gng, gnb, wbra, wbrc, st)


def _d2_kernel(pt_ref, q_ref, bias_ref, uo_ref, glater_ref, gall_ref, *refs):
    n = PAGES_PER_STEP
    k_refs, v_refs = refs[:n], refs[n:2 * n]
    o_ref, acc_scr, c_scr = refs[2 * n], refs[2 * n + 1], refs[2 * n + 2]
    j = pl.program_id(1)

    @pl.when(j == 0)
    def _():
        acc_scr[...] = jnp.zeros_like(acc_scr)
        c_scr[...] = jnp.zeros_like(c_scr)

    row_head = lax.broadcasted_iota(jnp.int32, (SB_HEADS, MIX_W), 0)
    col_head = lax.broadcasted_iota(jnp.int32, (SB_HEADS, MIX_W), 1) // SB_HD
    qbd = jnp.where(row_head == col_head, jnp.broadcast_to(q_ref[0], (SB_HEADS, MIX_W)), 0.0).astype(BF16)
    z = jnp.concatenate([_nn(qbd, k_refs[i][0].astype(BF16)) for i in range(n)], axis=0) + bias_ref[...]
    sp = _softplus(z)
    log_keep = -sp
    hi, lo = _split_bf16(log_keep)
    r2 = _nn(hi, uo_ref[...]) + _nn(lo, uo_ref[...])
    r, tot = r2[:, :PAGE], r2[:, PAGE:]
    t_hi, t_lo = _split_bf16(tot)
    later = _nn(glater_ref[...], t_hi) + _nn(glater_ref[...], t_lo)
    c = c_scr[...]
    w = jnp.exp((z - sp) + (r + (c + later)))
    c_scr[...] = c + _nn(gall_ref[...], t_hi) + _nn(gall_ref[...], t_lo)
    for h in range(SB_HEADS):
        rows = slice(h * SB_HD, (h + 1) * SB_HD)
        acc_h = acc_scr[rows, :]
        for i in range(n):
            acc_h = acc_h + v_refs[i][0, rows, :] * w[i * SB_HEADS + h:i * SB_HEADS + h + 1, :]
        acc_scr[rows, :] = acc_h

    @pl.when(j == pl.num_programs(1) - 1)
    def _():
        o_ref[0] = jnp.sum(acc_scr[...], axis=1, keepdims=True)


def _d2_call(layer, page_table, q, bias_rows, uo, glater, gall, cache_k_t, cache_v_t):
    nb, n_pages = page_table.shape
    n = PAGES_PER_STEP
    steps = n_pages // n

    def page_spec(i):
        return pl.BlockSpec((None, 1, MIX_W, PAGE),
                            lambda b, j, pt: (layer, pt[b, n_pages - 1 - (j * n + i)], 0, 0))

    const = lambda shape: pl.BlockSpec(shape, lambda b, j, pt: (0,) * len(shape))
    grid_spec = pltpu.PrefetchScalarGridSpec(
        num_scalar_prefetch=1,
        grid=(nb, steps),
        in_specs=[pl.BlockSpec((1, 1, MIX_W), lambda b, j, pt: (b, 0, 0)),
                  const(bias_rows.shape), const(uo.shape), const(glater.shape), const(gall.shape)]
                 + [page_spec(i) for i in range(n)] + [page_spec(i) for i in range(n)],
        out_specs=pl.BlockSpec((1, MIX_W, 1), lambda b, j, pt: (b, 0, 0)),
        scratch_shapes=[pltpu.VMEM((MIX_W, PAGE), F32), pltpu.VMEM((n * SB_HEADS, PAGE), F32)],
    )
    return pl.pallas_call(
        _d2_kernel,
        grid_spec=grid_spec,
        out_shape=jax.ShapeDtypeStruct((nb, MIX_W, 1), F32),
        compiler_params=pltpu.CompilerParams(dimension_semantics=("arbitrary", "arbitrary"),
                                             vmem_limit_bytes=VMEM_LIMIT),
        name="sample_paged_stick_breaking",
    )(page_table, q, bias_rows, uo, glater, gall, *([cache_k_t] * n), *([cache_v_t] * n))


def _d3_kernel(x_ref, mac_ref, gb_ref, yb_ref, p_ref, prev0_ref, prev1_ref, wbrb_ref, wout_ref, wup_ref, wdown_ref,
               wple_ref, wpg_ref, ln1g_ref, ln1b_ref, cw_ref, cb_ref, ln2g_ref, ln2b_ref,
               x2_ref, hup_ref):
    nb = x_ref.shape[0]
    merged = mac_ref[...] + gb_ref[...] * _nn(yb_ref[...].astype(BF16), wbrb_ref[...])
    h = _nn(merged.astype(BF16), wout_ref[...])
    x1 = _ln(DN_ALPHA * x_ref[...] + h, ln1g_ref[...], ln1b_ref[...])
    x1b = x1.astype(BF16)
    f = jnp.zeros((nb, D_MODEL), F32)
    for c0 in range(0, D_FF, FF_CH):
        halves = []
        for base in (c0, D_FF + c0):
            sl = slice(base, base + FF_CH)
            hup = _nn(x1b, wup_ref[:, sl])
            hup_ref[:, sl] = hup
            halves.append(cb_ref[:, sl] + cw_ref[0:1, sl] * prev0_ref[:, sl] + cw_ref[1:2, sl] * prev1_ref[:, sl]
                          + cw_ref[2:3, sl] * hup)
        act = (jax.nn.gelu(halves[0]) * halves[1]).astype(BF16)
        f = f + _nn(act, wdown_ref[c0:c0 + FF_CH, :])
    x2_ref[...] = _channel_mixer_tail(x1, f, p_ref[...].astype(BF16), wple_ref, wpg_ref, ln2g_ref, ln2b_ref)


def _d3_call(x, mac, gb, yb, p, prev0, prev1, wbrb, wout, wup, wdown, wple, wpg, ln1g, ln1b, cw, cb, ln2g, ln2b):
    nb = x.shape[0]
    return pl.pallas_call(
        _d3_kernel,
        out_shape=(jax.ShapeDtypeStruct((nb, D_MODEL), F32), jax.ShapeDtypeStruct((nb, D2), F32)),
        compiler_params=pltpu.CompilerParams(vmem_limit_bytes=VMEM_LIMIT),
        name="sample_channel_mixer",
    )(x, mac, gb, yb, p, prev0, prev1, wbrb, wout, wup, wdown, wple, wpg, ln1g, ln1b, cw, cb, ln2g, ln2b)


def _constants():
    tri = np.zeros((TM1, TM1), np.float32)
    for c in range(TM1 // GLA_CHUNK):
        r0 = c * GLA_CHUNK
        tri[r0:r0 + GLA_CHUNK, r0:r0 + GLA_CHUNK] = np.tril(np.ones((GLA_CHUNK, GLA_CHUNK), np.float32))
    jj = np.arange(TK)
    u_sb = (jj[:, None] > jj[None, :]).astype(np.float32)
    kk = np.arange(PAGE)
    uo = np.concatenate([(kk[:, None] > kk[None, :]).astype(np.float32), np.ones((PAGE, PAGE), np.float32)], axis=1)
    rows = np.arange(PAGES_PER_STEP * SB_HEADS)
    page_of, head_of = rows // SB_HEADS, rows % SB_HEADS
    same_head = head_of[:, None] == head_of[None, :]
    glater = (same_head & (page_of[None, :] < page_of[:, None])).astype(np.float32)
    gall = same_head.astype(np.float32)
    as_bf16 = lambda m: jnp.asarray(m, BF16)
    return as_bf16(tri), as_bf16(u_sb), as_bf16(uo), as_bf16(glater), as_bf16(gall)


def _row(vec):
    return vec.reshape(1, -1)


def kernel(x_prompt, x_sample, cache_k, cache_v, state_gla, state_conv, page_table, p_prompt, p_sample, w_in, b_in, gm_ws, gm_bs, gm_ln_g, gm_ln_b, sb_bias, gla_wa2, gla_ba, gla_gn_g, gla_gn_b, w_br_a, w_br_b, w_br_c, w_out, ln1_g, ln1_b, w_up, conv_w, conv_b, w_down, w_ple, w_pgate, ln2_g, ln2_b):
    bp, sp = x_prompt.shape[0], x_prompt.shape[1]
    bd = x_sample.shape[0]
    n_pool = cache_k.shape[1]
    tri, u_sb, uo, glater, gall = _constants()
    ck = jnp.transpose(cache_k, (0, 1, 3, 4, 2)).reshape(DEPTH, n_pool, MIX_W, PAGE)
    cv = jnp.transpose(cache_v, (0, 1, 3, 4, 2)).reshape(DEPTH, n_pool, MIX_W, PAGE)

    yp = x_prompt
    yd = x_sample.reshape(bd, D_MODEL)
    outs = [[] for _ in range(9)]
    for i in range(DEPTH):
        ga_pad = jnp.zeros((D_MODEL, LANES - GLA_RANK), F32)
        win = jnp.concatenate([w_in[i][:, :C_GA], w_in[i][:, C_GA:C_GA + GLA_RANK], ga_pad,
                               w_in[i][:, C_GA + GLA_RANK:]], axis=1).astype(BF16)
        bin_ = jnp.concatenate([b_in[i][:C_GA + GLA_RANK], jnp.zeros((LANES - GLA_RANK,), F32),
                                b_in[i][C_GA + GLA_RANK:]]).reshape(1, D_INP)
        wa2 = jnp.concatenate([gla_wa2[i], jnp.zeros((LANES - GLA_RANK, GLA_KW), F32)], axis=0).astype(BF16)
        bsf = jnp.repeat(gm_bs[i].T, LANES, axis=1)
        ws0 = _row(jnp.repeat(gm_ws[i][:, 0, 0], LANES))
        bs0 = _row(jnp.repeat(gm_bs[i][:, 0], LANES))
        gmg, gmb, ba = _row(gm_ln_g[i]), _row(gm_ln_b[i]), _row(gla_ba[i])
        gng, gnb = _row(gla_gn_g[i]), _row(gla_gn_b[i])
        wbra, wbrb, wbrc = w_br_a[i].astype(BF16), w_br_b[i].astype(BF16), w_br_c[i].astype(BF16)
        wout, wup, wdown = w_out[i].astype(BF16), w_up[i].astype(BF16), w_down[i].astype(BF16)
        wple, wpg = w_ple[i].astype(BF16), w_pgate[i].astype(BF16)
        ln1g, ln1b, ln2g, ln2b = _row(ln1_g[i]), _row(ln1_b[i]), _row(ln2_g[i]), _row(ln2_b[i])
        cw, cb = conv_w[i], _row(conv_b[i])

        kf, vf, qb, kb, vb, mac, gb, s2 = _p1_call(yp, win, bin_, gm_ws[i], bsf, gmg, gmb, wa2, ba, gng, gnb,
                                                    wbra, wbrc, tri)
        yb = _p2_call(sb_bias[i], qb, kb, vb, u_sb)
        yp, conv_p = _p3_call(i, yp, mac, gb, yb, p_prompt, wbrb, wout, wup, wdown, wple, wpg,
                              ln1g, ln1b, cw, cb, ln2g, ln2b)
        gla_p = s2.reshape(bp, GLA_DV, GLA_HEADS, GLA_DK).transpose(0, 2, 3, 1)

        st = state_gla[i].reshape(bd, GLA_KW, GLA_DV)
        kd, vd, qd, gmv, mac_d, gb_d, st_new = _d1_call(yd, win, bin_, ws0, bs0, gmg, gmb, wa2, ba, gng, gnb,
                                                         wbra, wbrc, st)
        bias_rows = jnp.broadcast_to(jnp.tile(sb_bias[i], PAGES_PER_STEP)[:, None],
                                     (PAGES_PER_STEP * SB_HEADS, PAGE))
        yb_d = _d2_call(i, page_table, qd.reshape(bd, 1, MIX_W), bias_rows, uo, glater, gall, ck, cv)
        yb_d = yb_d.reshape(bd, MIX_W)
        yd, hup_d = _d3_call(yd, mac_d, gb_d, yb_d, p_sample[i].reshape(bd, PLE_DIM),
                             state_conv[i][:, 0], state_conv[i][:, 1],
                             wbrb, wout, wup, wdown, wple, wpg, ln1g, ln1b, cw, cb, ln2g, ln2b)

        outs[0].append(kf.reshape(bp, sp, SB_HEADS, SB_HD))
        outs[1].append(vf.reshape(bp, sp, SB_HEADS, SB_HD))
        outs[2].append(kd.reshape(bd, 1, SB_HEADS, SB_HD))
        outs[3].append(vd.reshape(bd, 1, SB_HEADS, SB_HD))
        outs[4].append(gla_p)
        outs[5].append(st_new.reshape(bd, GLA_HEADS, GLA_DK, GLA_DV))
        outs[6].append(conv_p)
        outs[7].append(jnp.stack([state_conv[i][:, 1], hup_d], axis=1))
        outs[8].append(gmv.reshape(bd, 1, MIX_W))

    return (yp, yd.reshape(bd, 1, D_MODEL)) + tuple(jnp.stack(o) for o in outs)
```

```python
import functools

import numpy as np
import jax
import jax.numpy as jnp
from jax import lax
from jax.experimental import pallas as pl
from jax.experimental.pallas import tpu as pltpu

F32 = jnp.float32
BF16 = jnp.bfloat16

D_MODEL = 1024
DEPTH = 2
PAGE = 128
MIX_W = 512
GM_CHUNK = 128
GM_GROUPS = 4
SB_HEADS = 8
SB_HD = 64
GLA_HEADS = 4
GLA_DK = 64
GLA_DV = 128
GLA_KW = 256
GLA_RANK = 16
GLA_TAU = 16.0
GLA_CHUNK = 64
D_FF = 2816
D2 = 2 * D_FF
CONV_W = 3
PLE_DIM = 256
LN_EPS = 1e-5
DN_ALPHA = (2.0 * DEPTH) ** 0.25
LANES = 128
SUBLANES = 8

C_U, C_V, C_Q, C_K, C_VV = 0, 512, 1024, 1536, 2048
C_GQ, C_GK, C_GV, C_GR, C_GA, C_GATE = 2560, 2816, 3072, 3584, 4096, 4224
D_INP = C_GATE + 3 * D_MODEL

TM1 = 512
TM3 = 256
TQ = 512
TK = 256
SB_SLABS = 2
SB_KV_PER_BODY = 2
assert TQ % TK == 0 and (TQ // TK) % SB_KV_PER_BODY == 0
FF_CH = 256
PAGES_PER_STEP = 32
VMEM_LIMIT = 56 * 1024 * 1024


def _nt(a, b):
    return lax.dot_general(a, b, (((1,), (1,)), ((), ())), preferred_element_type=F32)


def _tn(a, b):
    return lax.dot_general(a, b, (((0,), (0,)), ((), ())), preferred_element_type=F32)


def _nn(a, b):
    return jnp.dot(a, b, preferred_element_type=F32)


def _ln(x, g, b):
    mu = jnp.mean(x, axis=-1, keepdims=True)
    xc = x - mu
    var = jnp.mean(xc * xc, axis=-1, keepdims=True)
    return xc * lax.rsqrt(var + LN_EPS) * g + b


def _softplus(z):
    return jnp.maximum(z, 0.0) + jnp.log1p(jnp.exp(-jnp.abs(z)))


def _log_sigmoid(z):
    return jnp.minimum(z, 0.0) - jnp.log1p(jnp.exp(-jnp.abs(z)))


def _split_bf16(x):
    hi = x.astype(BF16)
    lo = (x - hi.astype(F32)).astype(BF16)
    return hi, lo


def _const_spec(shape):
    nd = len(shape)
    return pl.BlockSpec(shape, lambda *_: (0,) * nd, pipeline_mode=pl.Buffered(1))


def _layer_spec(arr, layer):
    nd = arr.ndim - 1
    return pl.BlockSpec((None,) + arr.shape[1:], lambda *_: (layer,) + (0,) * nd, pipeline_mode=pl.Buffered(1))


def _full_spec(shape):
    nd = len(shape)
    return pl.BlockSpec(shape, lambda *_: (0,) * nd)


def _p1_kernel(x_ref, win_ref, bin_ref, ws_ref, bsf_ref, gmg_ref, gmb_ref, wa2_ref, ba_ref, gng_ref, gnb_ref,
               wbra_ref, wbrc_ref, tri_ref,
               kf_ref, vf_ref, qb_ref, kb_ref, vb_ref, mac_ref, gb_ref, st_ref, s2_scr):
    tm = x_ref.shape[1]

    @pl.when(pl.program_id(1) == 0)
    def _():
        s2_scr[...] = jnp.zeros_like(s2_scr)

    xb = x_ref[0].astype(BF16)

    def proj(c0, c1):
        return _nn(xb, win_ref[:, c0:c1]) + bin_ref[:, c0:c1]

    q = proj(C_Q, C_K)
    k = proj(C_K, C_VV)
    v = proj(C_VV, C_GQ)
    kf_ref[0] = k
    vf_ref[0] = v
    qb_ref[0] = (q * (SB_HD ** -0.5)).astype(BF16)
    kb_ref[0] = k.astype(BF16)
    vb_ref[0] = v.astype(BF16)

    u = jax.nn.gelu(proj(C_U, C_V))
    vn = _ln(jax.nn.gelu(proj(C_V, C_Q)), gmg_ref[...], gmb_ref[...])
    vnb = vn.astype(BF16)
    r_i = lax.broadcasted_iota(jnp.int32, (GM_CHUNK, GM_CHUNK), 0)
    c_i = lax.broadcasted_iota(jnp.int32, (GM_CHUNK, GM_CHUNK), 1)
    ws_m = [jnp.where(r_i >= c_i, ws_ref[g], 0.0).astype(BF16) for g in range(GM_GROUPS)]
    sv_rows = []
    for n in range(tm // GM_CHUNK):
        cols = [_nn(ws_m[g], vnb[n * GM_CHUNK:(n + 1) * GM_CHUNK, g * LANES:(g + 1) * LANES])
                for g in range(GM_GROUPS)]
        sv_rows.append(jnp.concatenate(cols, axis=1) + bsf_ref[...])
    y_a = u * jnp.concatenate(sv_rows, axis=0)

    gq = proj(C_GQ, C_GK) * (GLA_DK ** -0.5)
    gk = proj(C_GK, C_GV)
    gv = proj(C_GV, C_GR).astype(BF16)
    ga = proj(C_GA, C_GATE).astype(BF16)
    la = _log_sigmoid(_nn(ga, wa2_ref[...]) + ba_ref[...]) * (1.0 / GLA_TAU)
    la_hi, la_lo = _split_bf16(la)
    tri = tri_ref[...]
    bcum = _nn(tri, la_hi) + _nn(tri, la_lo)
    head_of_lane = lax.broadcasted_iota(jnp.int32, (GLA_CHUNK, GLA_KW), 1) // GLA_DK
    t_i = lax.broadcasted_iota(jnp.int32, (GLA_CHUNK, GLA_CHUNK), 0)
    s_i = lax.broadcasted_iota(jnp.int32, (GLA_CHUNK, GLA_CHUNK), 1)
    causal = s_i <= t_i
    n_chunks = tm // GLA_CHUNK
    heads = range(GLA_HEADS)
    masks = [head_of_lane == h for h in heads]
    qe_m, v_h, decay, att, upd = [], [], [], [], []
    for c in range(n_chunks):
        r0 = c * GLA_CHUNK
        bc = bcum[r0:r0 + GLA_CHUNK]
        bref = bc[GLA_CHUNK // 2:GLA_CHUNK // 2 + 1]
        blast = bc[GLA_CHUNK - 1:GLA_CHUNK]
        qc = gq[r0:r0 + GLA_CHUNK]
        kc = gk[r0:r0 + GLA_CHUNK]
        vc = gv[r0:r0 + GLA_CHUNK]
        qe = qc * jnp.exp(bc)
        qt = qc * jnp.exp(bc - bref)
        kt = (kc * jnp.exp(bref - bc)).astype(BF16)
        kd = kc * jnp.exp(blast - bc)
        qe_m.append([jnp.where(masks[h], qe, 0.0).astype(BF16) for h in heads])
        v_h.append([vc[:, h * GLA_DV:(h + 1) * GLA_DV] for h in heads])
        decay.append(jnp.exp(blast))
        att.append([_nt(jnp.where(masks[h], qt, 0.0).astype(BF16), kt) for h in heads])
        u_c = _tn(v_h[c][0], jnp.where(masks[0], kd, 0.0).astype(BF16))
        for h in range(1, GLA_HEADS):
            u_c = u_c + _tn(v_h[c][h], jnp.where(masks[h], kd, 0.0).astype(BF16))
        upd.append(u_c)
    o_intra = [[_nn(jnp.where(causal, att[c][h], 0.0).astype(BF16), v_h[c][h]) for h in heads]
               for c in range(n_chunks)]
    state = s2_scr[...]
    states = []
    for c in range(n_chunks):
        states.append(state.astype(BF16))
        state = state * decay[c] + upd[c]
    s2_scr[...] = state
    st_ref[0] = state
    o_c = jnp.concatenate(
        [jnp.concatenate([o_intra[c][h] + _nt(qe_m[c][h], states[c]) for h in heads], axis=1)
         for c in range(n_chunks)], axis=0)
    o_n = jnp.concatenate(
        [_ln(o_c[:, h * GLA_DV:(h + 1) * GLA_DV], gng_ref[:, h * GLA_DV:(h + 1) * GLA_DV],
             gnb_ref[:, h * GLA_DV:(h + 1) * GLA_DV]) for h in range(GLA_HEADS)], axis=1)
    y_c = jax.nn.silu(proj(C_GR, C_GA)) * o_n

    g_a = jax.nn.sigmoid(proj(C_GATE, C_GATE + D_MODEL))
    g_c = jax.nn.sigmoid(proj(C_GATE + 2 * D_MODEL, C_GATE + 3 * D_MODEL))
    mac_ref[0] = g_a * _nn(y_a.astype(BF16), wbra_ref[...]) + g_c * _nn(y_c.astype(BF16), wbrc_ref[...])
    gb_ref[0] = jax.nn.sigmoid(proj(C_GATE + D_MODEL, C_GATE + 2 * D_MODEL))


def _p1_call(layer, x, params, tri):
    bsz, seq, _ = x.shape
    tm = TM1
    tile = lambda w: pl.BlockSpec((1, tm, w), lambda b, s: (b, s, 0))
    out_shape = (
        jax.ShapeDtypeStruct((bsz, seq, MIX_W), F32), jax.ShapeDtypeStruct((bsz, seq, MIX_W), F32),
        jax.ShapeDtypeStruct((bsz, seq, MIX_W), BF16), jax.ShapeDtypeStruct((bsz, seq, MIX_W), BF16),
        jax.ShapeDtypeStruct((bsz, seq, MIX_W), BF16),
        jax.ShapeDtypeStruct((bsz, seq, D_MODEL), F32), jax.ShapeDtypeStruct((bsz, seq, D_MODEL), F32),
        jax.ShapeDtypeStruct((bsz, GLA_DV, GLA_KW), F32),
    )
    out_specs = (tile(MIX_W), tile(MIX_W), tile(MIX_W), tile(MIX_W), tile(MIX_W), tile(D_MODEL), tile(D_MODEL),
                 pl.BlockSpec((1, GLA_DV, GLA_KW), lambda b, s: (b, 0, 0)))
    return pl.pallas_call(
        _p1_kernel,
        grid=(bsz, seq // tm),
        in_specs=[tile(D_MODEL)] + [_layer_spec(p, layer) for p in params] + [_const_spec(tri.shape)],
        out_specs=out_specs,
        out_shape=out_shape,
        scratch_shapes=[pltpu.VMEM((GLA_DV, GLA_KW), F32)],
        compiler_params=pltpu.CompilerParams(dimension_semantics=("arbitrary", "arbitrary"),
                                             vmem_limit_bytes=VMEM_LIMIT),
        name="prompt_proj_mixers",
    )(x, *params, tri)


def _p2_kernel(bias_ref, q_ref, k_ref, v_ref, u2_ref, o_ref):
    tq = q_ref.shape[1]
    tk = TK
    rs = tq // SB_SLABS
    n_diag = tq // tk
    hp = pl.program_id(1)
    i = pl.program_id(2)
    lane = lax.broadcasted_iota(jnp.int32, (rs, LANES), 1)
    t_i = lax.broadcasted_iota(jnp.int32, (rs, tk), 0)
    s_i = lax.broadcasted_iota(jnp.int32, (rs, tk), 1)
    u2 = u2_ref[...]
    sign_bit = jnp.uint32(0x80000000)
    chains = [(hh, sl) for sl in range(SB_SLABS) for hh in range(2)]
    q_of = []
    for hh, sl in chains:
        q_rows = q_ref[0, sl * rs:(sl + 1) * rs, :]
        q_of.append(jnp.where((lane < SB_HD) if hh == 0 else (lane >= SB_HD), q_rows, jnp.zeros_like(q_rows)))
    biases = (bias_ref[2 * hp], bias_ref[2 * hp + 1])

    def blocks(js, carry, key_offs):
        kblk = [k_ref[0, pl.ds(pl.multiple_of(j * tk, tk), tk), :] for j in js]
        vblk = [v_ref[0, pl.ds(pl.multiple_of(j * tk, tk), tk), :] for j in js]
        masks, live = {}, []
        for b, key_off in enumerate(key_offs):
            for idx, (hh, sl) in enumerate(chains):
                if key_off is not None and key_off + 1 >= (sl + 1) * rs:
                    continue
                live.append((b, idx))
                if key_off is not None and key_off + tk > sl * rs:
                    masks[b, idx] = s_i + key_off < t_i + sl * rs
        zs = {e: _nt(q_of[e[1]], kblk[e[0]]) for e in live}
        log_w0s, log_keeps, splits = {}, {}, {}
        for e in live:
            z = zs[e] + biases[chains[e[1]][0]]
            neg_abs = lax.bitcast_convert_type(lax.bitcast_convert_type(z, jnp.uint32) | sign_bit, F32)
            s = jnp.log(1.0 + jnp.exp(neg_abs))
            log_w0 = jnp.minimum(z, 0.0) - s
            log_keep = log_w0 - z
            if e in masks:
                log_keep = jnp.where(masks[e], log_keep, 0.0)
            log_w0s[e], log_keeps[e] = log_w0, log_keep
            splits[e] = log_keep.astype(BF16)
        sums = {e: _nn(splits[e], u2) for e in live}
        c_run = [carry[idx][0] for idx in range(len(chains))]
        ws = {}
        for e in live:
            b, idx = e
            w = jnp.exp(log_w0s[e] + (sums[e] + c_run[idx]))
            if e in masks:
                w = jnp.where(masks[e], w, 0.0)
            ws[e] = w.astype(BF16)
            c_run[idx] = c_run[idx] + (sums[e][:, 0:1] + log_keeps[e][:, 0:1])
        acc = [carry[idx][1] for idx in range(len(chains))]
        for e in live:
            acc[e[1]] = acc[e[1]] + _nn(ws[e], vblk[e[0]])
        return tuple((c_run[idx], acc[idx]) for idx in range(len(chains)))

    carry = tuple((jnp.zeros((rs, 1), F32), jnp.zeros((rs, LANES), F32)) for _ in chains)
    carry = blocks([i * n_diag + (n_diag - 1 - d) for d in range(n_diag)], carry,
                   [(n_diag - 1 - d) * tk for d in range(n_diag)])
    n_groups = (i * n_diag) // SB_KV_PER_BODY

    def body(it, cr):
        top = (n_groups - it) * SB_KV_PER_BODY - 1
        return blocks([top - b for b in range(SB_KV_PER_BODY)], cr, [None] * SB_KV_PER_BODY)

    carry = lax.fori_loop(0, n_groups, body, carry)
    for sl in range(SB_SLABS):
        o_ref[0, sl * rs:(sl + 1) * rs, :] = jnp.where(lane < SB_HD, carry[2 * sl][1],
                                                       carry[2 * sl + 1][1]).astype(o_ref.dtype)


def _p2_call(bias, qb, kb, vb, u):
    bsz, seq, _ = qb.shape
    n_hp = MIX_W // LANES
    return pl.pallas_call(
        _p2_kernel,
        grid=(bsz, n_hp, seq // TQ),
        in_specs=[pl.BlockSpec(memory_space=pltpu.SMEM),
                  pl.BlockSpec((1, TQ, LANES), lambda b, h, i: (b, i, h)),
                  pl.BlockSpec((1, seq, LANES), lambda b, h, i: (b, 0, h)),
                  pl.BlockSpec((1, seq, LANES), lambda b, h, i: (b, 0, h)),
                  pl.BlockSpec(u.shape, lambda b, h, i: (0, 0))],
        out_specs=pl.BlockSpec((1, TQ, LANES), lambda b, h, i: (b, i, h)),
        out_shape=jax.ShapeDtypeStruct((bsz, seq, MIX_W), BF16),
        compiler_params=pltpu.CompilerParams(dimension_semantics=("arbitrary", "arbitrary", "arbitrary"),
                                             vmem_limit_bytes=VMEM_LIMIT),
        name="prompt_stick_breaking",
    )(bias, qb, kb, vb, u)


def _channel_mixer_tail(x1, f, pb, wple_ref, wpg_ref, ln2g_ref, ln2b_ref):
    x1b = x1.astype(BF16)
    e = _nn(pb, wple_ref[...]) * jax.nn.sigmoid(_nn(x1b, wpg_ref[...]))
    return _ln(DN_ALPHA * x1 + f + e, ln2g_ref[...], ln2b_ref[...])


def _p3_kernel(x_ref, mac_ref, gb_ref, yb_ref, p_ref, wbrb_ref, wout_ref, wup_ref, wdown_ref, wple_ref, wpg_ref,
               ln1g_ref, ln1b_ref, cw_ref, cb_ref, ln2g_ref, ln2b_ref,
               x2_ref, conv_ref, hs_scr):
    tm = x_ref.shape[1]
    pad = SUBLANES

    @pl.when(pl.program_id(1) == 0)
    def _():
        hs_scr[0:pad, :] = jnp.zeros((pad, D2), F32)

    merged = mac_ref[0] + gb_ref[0] * _nn(yb_ref[0], wbrb_ref[...])
    h = _nn(merged.astype(BF16), wout_ref[...])
    x1 = _ln(DN_ALPHA * x_ref[0] + h, ln1g_ref[...], ln1b_ref[...])
    x1b = x1.astype(BF16)

    def up(c0):
        return [_nn(x1b, wup_ref[:, base:base + FF_CH]) for base in (c0, D_FF + c0)]

    def conv_act(c0, hups):
        halves = []
        for base, hup in zip((c0, D_FF + c0), hups):
            cols = slice(base, base + FF_CH)
            hs_scr[pad:pad + tm, cols] = hup
            hc = cb_ref[:, cols] + cw_ref[CONV_W - 1:CONV_W, cols] * hup
            for i in range(CONV_W - 1):
                hc = hc + cw_ref[i:i + 1, cols] * hs_scr[pad - 2 + i:pad - 2 + i + tm, cols]
            halves.append(hc)
        return (jax.nn.gelu(halves[0]) * halves[1]).astype(BF16)

    chunks = list(range(0, D_FF, FF_CH))
    f = jnp.zeros((tm, D_MODEL), F32)
    hups = up(chunks[0])
    act_prev = None
    for n, c0 in enumerate(chunks):
        if act_prev is not None:
            f = f + _nn(act_prev, wdown_ref[chunks[n - 1]:chunks[n - 1] + FF_CH, :])
        hups_next = up(chunks[n + 1]) if n + 1 < len(chunks) else None
        act_prev = conv_act(c0, hups)
        hups = hups_next
    f = f + _nn(act_prev, wdown_ref[chunks[-1]:chunks[-1] + FF_CH, :])
    last = hs_scr[pad + tm - 2:pad + tm, :]
    hs_scr[pad - 2:pad, :] = last
    conv_ref[0] = last

    x2_ref[0] = _channel_mixer_tail(x1, f, p_ref[0].astype(BF16), wple_ref, wpg_ref, ln2g_ref, ln2b_ref)


def _p3_call(layer, x, mac, gb, yb, p_all, params):
    bsz, seq, _ = x.shape
    tm = TM3
    tile = lambda w: pl.BlockSpec((1, tm, w), lambda b, s: (b, s, 0))
    return pl.pallas_call(
        _p3_kernel,
        grid=(bsz, seq // tm),
        in_specs=[tile(D_MODEL), tile(D_MODEL), tile(D_MODEL), tile(MIX_W),
                  pl.BlockSpec((None, 1, tm, PLE_DIM), lambda b, s: (layer, b, s, 0))]
                 + [_layer_spec(p, layer) for p in params],
        out_specs=(tile(D_MODEL), pl.BlockSpec((1, CONV_W - 1, D2), lambda b, s: (b, 0, 0))),
        out_shape=(jax.ShapeDtypeStruct((bsz, seq, D_MODEL), F32),
                   jax.ShapeDtypeStruct((bsz, CONV_W - 1, D2), F32)),
        scratch_shapes=[pltpu.VMEM((tm + SUBLANES, D2), F32)],
        compiler_params=pltpu.CompilerParams(dimension_semantics=("arbitrary", "arbitrary"),
                                             vmem_limit_bytes=VMEM_LIMIT),
        name="prompt_channel_mixer",
    )(x, mac, gb, yb, p_all, *params)


def _d1_kernel(x_ref, win_ref, bin_ref, ws0_ref, bs0_ref, gmg_ref, gmb_ref, wa2_ref, ba_ref, gng_ref, gnb_ref,
               wbra_ref, wbrc_ref, st_ref,
               kf_ref, vf_ref, qf_ref, gmv_ref, mac_ref, gb_ref, stn_ref, oc_scr):
    nb = x_ref.shape[0]
    xb = x_ref[...].astype(BF16)

    def proj(c0, c1):
        return _nn(xb, win_ref[:, c0:c1]) + bin_ref[:, c0:c1]

    qf_ref[...] = proj(C_Q, C_K) * (SB_HD ** -0.5)
    kf_ref[...] = proj(C_K, C_VV)
    vf_ref[...] = proj(C_VV, C_GQ)

    u = jax.nn.gelu(proj(C_U, C_V))
    vn = _ln(jax.nn.gelu(proj(C_V, C_Q)), gmg_ref[...], gmb_ref[...])
    gmv_ref[...] = vn
    y_a = u * (ws0_ref[...] * vn + bs0_ref[...])

    gq = proj(C_GQ, C_GK) * (GLA_DK ** -0.5)
    gk = proj(C_GK, C_GV)
    gv = proj(C_GV, C_GR)
    ga = proj(C_GA, C_GATE).astype(BF16)
    a = jnp.exp(_log_sigmoid(_nn(ga, wa2_ref[...]) + ba_ref[...]) * (1.0 / GLA_TAU))

    def cols(m):
        return jnp.concatenate([m, jnp.zeros((LANES - nb, GLA_KW), F32)], axis=0).T

    a_t, k_t, q_t = cols(a), cols(gk), cols(gq)
    for b in range(nb):
        s_old = st_ref[b]
        v_exp = jnp.concatenate(
            [jnp.broadcast_to(gv[b:b + 1, h * GLA_DV:(h + 1) * GLA_DV], (GLA_DK, GLA_DV)) for h in range(GLA_HEADS)],
            axis=0)
        s_new = a_t[:, b:b + 1] * s_old + k_t[:, b:b + 1] * v_exp
        stn_ref[b] = s_new
        oq = q_t[:, b:b + 1] * s_new
        for h in range(GLA_HEADS):
            oc_scr[b:b + 1, h * GLA_DV:(h + 1) * GLA_DV] = jnp.sum(oq[h * GLA_DK:(h + 1) * GLA_DK], axis=0,
                                                                     keepdims=True)
    o_c = oc_scr[...]
    o_n = jnp.concatenate(
        [_ln(o_c[:, h * GLA_DV:(h + 1) * GLA_DV], gng_ref[:, h * GLA_DV:(h + 1) * GLA_DV],
             gnb_ref[:, h * GLA_DV:(h + 1) * GLA_DV]) for h in range(GLA_HEADS)], axis=1)
    y_c = jax.nn.silu(proj(C_GR, C_GA)) * o_n

    g_a = jax.nn.sigmoid(proj(C_GATE, C_GATE + D_MODEL))
    g_c = jax.nn.sigmoid(proj(C_GATE + 2 * D_MODEL, C_GATE + 3 * D_MODEL))
    mac_ref[...] = g_a * _nn(y_a.astype(BF16), wbra_ref[...]) + g_c * _nn(y_c.astype(BF16), wbrc_ref[...])
    gb_ref[...] = jax.nn.sigmoid(proj(C_GATE + D_MODEL, C_GATE + 2 * D_MODEL))


def _d1_call(layer, x, params, st_all):
    nb = x.shape[0]
    out_shape = (
        jax.ShapeDtypeStruct((nb, MIX_W), F32), jax.ShapeDtypeStruct((nb, MIX_W), F32),
        jax.ShapeDtypeStruct((nb, MIX_W), F32), jax.ShapeDtypeStruct((nb, MIX_W), F32),
        jax.ShapeDtypeStruct((nb, D_MODEL), F32), jax.ShapeDtypeStruct((nb, D_MODEL), F32),
        jax.ShapeDtypeStruct((nb, GLA_KW, GLA_DV), F32),
    )
    return pl.pallas_call(
        _d1_kernel,
        grid=(1,),
        in_specs=[_full_spec(x.shape)] + [_layer_spec(p, layer) for p in params] + [_layer_spec(st_all, layer)],
        out_specs=tuple(_full_spec(o.shape) for o in out_shape),
        out_shape=out_shape,
        scratch_shapes=[pltpu.VMEM((nb, MIX_W), F32)],
        compiler_params=pltpu.CompilerParams(dimension_semantics=("arbitrary",), vmem_limit_bytes=VMEM_LIMIT),
        name="sample_proj_mixers",
    )(x, *params, st_all)


def _d2_kernel(pt_ref, q_ref, bias_ref, uo_ref, glater_ref, gall_ref, *refs):
    n = PAGES_PER_STEP
    k_refs, v_refs = refs[:n], refs[n:2 * n]
    o_ref, acc_scr, c_scr = refs[2 * n], refs[2 * n + 1], refs[2 * n + 2]
    j = pl.program_id(1)

    @pl.when(j == 0)
    def _():
        acc_scr[...] = jnp.zeros_like(acc_scr)
        c_scr[...] = jnp.zeros_like(c_scr)

    row_head = lax.broadcasted_iota(jnp.int32, (SB_HEADS, MIX_W), 0)
    col_head = lax.broadcasted_iota(jnp.int32, (SB_HEADS, MIX_W), 1) // SB_HD
    qbd = jnp.where(row_head == col_head, jnp.broadcast_to(q_ref[0], (SB_HEADS, MIX_W)), 0.0).astype(BF16)
    z = jnp.concatenate([_nn(qbd, k_refs[i][0].astype(BF16)) for i in range(n)], axis=0) + bias_ref[...]
    sp = _softplus(z)
    log_keep = -sp
    hi, lo = _split_bf16(log_keep)
    r2 = _nn(hi, uo_ref[...]) + _nn(lo, uo_ref[...])
    r, tot = r2[:, :PAGE], r2[:, PAGE:]
    t_hi, t_lo = _split_bf16(tot)
    later = _nn(glater_ref[...], t_hi) + _nn(glater_ref[...], t_lo)
    c = c_scr[...]
    w = jnp.exp((z - sp) + (r + (c + later)))
    c_scr[...] = c + _nn(gall_ref[...], t_hi) + _nn(gall_ref[...], t_lo)
    for h in range(SB_HEADS):
        rows = slice(h * SB_HD, (h + 1) * SB_HD)
        acc_h = acc_scr[rows, :]
        for i in range(n):
            acc_h = acc_h + v_refs[i][0, rows, :] * w[i * SB_HEADS + h:i * SB_HEADS + h + 1, :]
        acc_scr[rows, :] = acc_h

    @pl.when(j == pl.num_programs(1) - 1)
    def _():
        o_ref[0] = jnp.sum(acc_scr[...], axis=1, keepdims=True)


def _d2_call(layer, page_table, q, bias_rows, uo, glater, gall, cache_k_t, cache_v_t):
    nb, n_pages = page_table.shape
    n = PAGES_PER_STEP
    steps = n_pages // n

    def page_spec(i):
        return pl.BlockSpec((None, 1, MIX_W, PAGE),
                            lambda b, j, pt: (layer, pt[b, n_pages - 1 - (j * n + i)], 0, 0))

    const = lambda shape: pl.BlockSpec(shape, lambda b, j, pt: (0,) * len(shape))
    grid_spec = pltpu.PrefetchScalarGridSpec(
        num_scalar_prefetch=1,
        grid=(nb, steps),
        in_specs=[pl.BlockSpec((1, 1, MIX_W), lambda b, j, pt: (b, 0, 0)),
                  const(bias_rows.shape), const(uo.shape), const(glater.shape), const(gall.shape)]
                 + [page_spec(i) for i in range(n)] + [page_spec(i) for i in range(n)],
        out_specs=pl.BlockSpec((1, MIX_W, 1), lambda b, j, pt: (b, 0, 0)),
        scratch_shapes=[pltpu.VMEM((MIX_W, PAGE), F32), pltpu.VMEM((n * SB_HEADS, PAGE), F32)],
    )
    return pl.pallas_call(
        _d2_kernel,
        grid_spec=grid_spec,
        out_shape=jax.ShapeDtypeStruct((nb, MIX_W, 1), F32),
        compiler_params=pltpu.CompilerParams(dimension_semantics=("arbitrary", "arbitrary"),
                                             vmem_limit_bytes=VMEM_LIMIT),
        name="sample_paged_stick_breaking",
    )(page_table, q, bias_rows, uo, glater, gall, *([cache_k_t] * n), *([cache_v_t] * n))


def _d3_kernel(x_ref, mac_ref, gb_ref, yb_ref, p_ref, prev0_ref, prev1_ref, wbrb_ref, wout_ref, wup_ref, wdown_ref,
               wple_ref, wpg_ref, ln1g_ref, ln1b_ref, cw_ref, cb_ref, ln2g_ref, ln2b_ref,
               x2_ref, hup_ref):
    nb = x_ref.shape[0]
    merged = mac_ref[...] + gb_ref[...] * _nn(yb_ref[...].astype(BF16), wbrb_ref[...])
    h = _nn(merged.astype(BF16), wout_ref[...])
    x1 = _ln(DN_ALPHA * x_ref[...] + h, ln1g_ref[...], ln1b_ref[...])
    x1b = x1.astype(BF16)
    f = jnp.zeros((nb, D_MODEL), F32)
    for c0 in range(0, D_FF, FF_CH):
        halves = []
        for base in (c0, D_FF + c0):
            sl = slice(base, base + FF_CH)
            hup = _nn(x1b, wup_ref[:, sl])
            hup_ref[:, sl] = hup
            halves.append(cb_ref[:, sl] + cw_ref[0:1, sl] * prev0_ref[:, sl] + cw_ref[1:2, sl] * prev1_ref[:, sl]
                          + cw_ref[2:3, sl] * hup)
        act = (jax.nn.gelu(halves[0]) * halves[1]).astype(BF16)
        f = f + _nn(act, wdown_ref[c0:c0 + FF_CH, :])
    x2_ref[...] = _channel_mixer_tail(x1, f, p_ref[...].astype(BF16), wple_ref, wpg_ref, ln2g_ref, ln2b_ref)


def _d3_call(layer, x, mac, gb, yb, p_all, prev0, prev1, params):
    nb = x.shape[0]
    acts = (x, mac, gb, yb)
    out_shape = (jax.ShapeDtypeStruct((nb, D_MODEL), F32), jax.ShapeDtypeStruct((nb, D2), F32))
    return pl.pallas_call(
        _d3_kernel,
        grid=(1,),
        in_specs=[_full_spec(a.shape) for a in acts] + [_layer_spec(p_all, layer)]
                 + [_full_spec(prev0.shape), _full_spec(prev1.shape)] + [_layer_spec(p, layer) for p in params],
        out_specs=tuple(_full_spec(o.shape) for o in out_shape),
        out_shape=out_shape,
        compiler_params=pltpu.CompilerParams(dimension_semantics=("arbitrary",), vmem_limit_bytes=VMEM_LIMIT),
        name="sample_channel_mixer",
    )(*acts, p_all, prev0, prev1, *params)


def _constants():
    tri = np.zeros((TM1, TM1), np.float32)
    for c in range(TM1 // GLA_CHUNK):
        r0 = c * GLA_CHUNK
        tri[r0:r0 + GLA_CHUNK, r0:r0 + GLA_CHUNK] = np.tril(np.ones((GLA_CHUNK, GLA_CHUNK), np.float32))
    jj = np.arange(TK)
    u_sb = (jj[:, None] > jj[None, :]).astype(np.float32)
    kk = np.arange(PAGE)
    uo = np.concatenate([(kk[:, None] > kk[None, :]).astype(np.float32), np.ones((PAGE, PAGE), np.float32)], axis=1)
    rows = np.arange(PAGES_PER_STEP * SB_HEADS)
    page_of, head_of = rows // SB_HEADS, rows % SB_HEADS
    same_head = head_of[:, None] == head_of[None, :]
    glater = (same_head & (page_of[None, :] < page_of[:, None])).astype(np.float32)
    gall = same_head.astype(np.float32)
    as_bf16 = lambda m: jnp.asarray(m, BF16)
    return as_bf16(tri), as_bf16(u_sb), as_bf16(uo), as_bf16(glater), as_bf16(gall)


def _row(vec):
    return vec.reshape(1, -1)


def kernel(x_prompt, x_sample, cache_k, cache_v, state_gla, state_conv, page_table, p_prompt, p_sample, w_in, b_in, gm_ws, gm_bs, gm_ln_g, gm_ln_b, sb_bias, gla_wa2, gla_ba, gla_gn_g, gla_gn_b, w_br_a, w_br_b, w_br_c, w_out, ln1_g, ln1_b, w_up, conv_w, conv_b, w_down, w_ple, w_pgate, ln2_g, ln2_b):
    bp, sp = x_prompt.shape[0], x_prompt.shape[1]
    bd = x_sample.shape[0]
    n_pool = cache_k.shape[1]
    tri, u_sb, uo, glater, gall = _constants()
    ck = jnp.transpose(cache_k, (0, 1, 3, 4, 2)).reshape(DEPTH, n_pool, MIX_W, PAGE)
    cv = jnp.transpose(cache_v, (0, 1, 3, 4, 2)).reshape(DEPTH, n_pool, MIX_W, PAGE)

    ga_pad = jnp.zeros((DEPTH, D_MODEL, LANES - GLA_RANK), F32)
    split = C_GA + GLA_RANK
    win = jnp.concatenate([w_in[:, :, :split], ga_pad, w_in[:, :, split:]], axis=2).astype(BF16)
    bin_ = jnp.concatenate([b_in[:, :split], ga_pad[:, 0], b_in[:, split:]], axis=1).reshape(DEPTH, 1, D_INP)
    wa2 = jnp.concatenate([gla_wa2, jnp.zeros((DEPTH, LANES - GLA_RANK, GLA_KW), F32)], axis=1).astype(BF16)
    bsf = jnp.repeat(jnp.swapaxes(gm_bs, 1, 2), LANES, axis=2)
    ws0 = jnp.repeat(gm_ws[:, :, 0, 0], LANES, axis=1).reshape(DEPTH, 1, MIX_W)
    bs0 = jnp.repeat(gm_bs[:, :, 0], LANES, axis=1).reshape(DEPTH, 1, MIX_W)
    rows = lambda v: v.reshape(DEPTH, 1, -1)
    mixer_tail = (rows(gm_ln_g), rows(gm_ln_b), wa2, rows(gla_ba), rows(gla_gn_g), rows(gla_gn_b),
                  w_br_a.astype(BF16), w_br_c.astype(BF16))
    p1_params = (win, bin_, gm_ws, bsf) + mixer_tail
    d1_params = (win, bin_, ws0, bs0) + mixer_tail
    p3_params = (w_br_b.astype(BF16), w_out.astype(BF16), w_up.astype(BF16), w_down.astype(BF16),
                 w_ple.astype(BF16), w_pgate.astype(BF16), rows(ln1_g), rows(ln1_b), conv_w, rows(conv_b),
                 rows(ln2_g), rows(ln2_b))
    st_all = state_gla.reshape(DEPTH, bd, GLA_KW, GLA_DV)
    p_sample_all = p_sample.reshape(DEPTH, bd, PLE_DIM)

    yp = x_prompt
    yd = x_sample.reshape(bd, D_MODEL)
    outs = [[] for _ in range(9)]
    for i in range(DEPTH):
        kf, vf, qb, kb, vb, mac, gb, s2 = _p1_call(i, yp, p1_params, tri)
        yb = _p2_call(sb_bias[i], qb, kb, vb, u_sb)
        yp, conv_p = _p3_call(i, yp, mac, gb, yb, p_prompt, p3_params)
        gla_p = s2.reshape(bp, GLA_DV, GLA_HEADS, GLA_DK).transpose(0, 2, 3, 1)

        kd, vd, qd, gmv, mac_d, gb_d, st_new = _d1_call(i, yd, d1_params, st_all)
        bias_rows = jnp.broadcast_to(jnp.tile(sb_bias[i], PAGES_PER_STEP)[:, None],
                                     (PAGES_PER_STEP * SB_HEADS, PAGE))
        yb_d = _d2_call(i, page_table, qd.reshape(bd, 1, MIX_W), bias_rows, uo, glater, gall, ck, cv)
        yb_d = yb_d.reshape(bd, MIX_W)
        yd, hup_d = _d3_call(i, yd, mac_d, gb_d, yb_d, p_sample_all, state_conv[i][:, 0], state_conv[i][:, 1],
                             p3_params)

        outs[0].append(kf.reshape(bp, sp, SB_HEADS, SB_HD))
        outs[1].append(vf.reshape(bp, sp, SB_HEADS, SB_HD))
        outs[2].append(kd.reshape(bd, 1, SB_HEADS, SB_HD))
        outs[3].append(vd.reshape(bd, 1, SB_HEADS, SB_HD))
        outs[4].append(gla_p)
        outs[5].append(st_new.reshape(bd, GLA_HEADS, GLA_DK, GLA_DV))
        outs[6].append(conv_p)
        outs[7].append(jnp.stack([state_conv[i][:, 1], hup_d], axis=1))
        outs[8].append(gmv.reshape(bd, 1, MIX_W))

    return (yp, yd.reshape(bd, 1, D_MODEL)) + tuple(jnp.stack(o) for o in outs)
```

```python
import functools

import numpy as np
import jax
import jax.numpy as jnp
from jax import lax
from jax.experimental import pallas as pl
from jax.experimental.pallas import tpu as pltpu

F32 = jnp.float32
BF16 = jnp.bfloat16

D_MODEL = 1024
DEPTH = 2
PAGE = 128
MIX_W = 512
GM_CHUNK = 128
GM_GROUPS = 4
SB_HEADS = 8
SB_HD = 64
GLA_HEADS = 4
GLA_DK = 64
GLA_DV = 128
GLA_KW = 256
GLA_RANK = 16
GLA_TAU = 16.0
GLA_CHUNK = 64
D_FF = 2816
D2 = 2 * D_FF
CONV_W = 3
PLE_DIM = 256
LN_EPS = 1e-5
DN_ALPHA = (2.0 * DEPTH) ** 0.25
LANES = 128
SUBLANES = 8

C_U, C_V, C_Q, C_K, C_VV = 0, 512, 1024, 1536, 2048
C_GQ, C_GK, C_GV, C_GR, C_GA, C_GATE = 2560, 2816, 3072, 3584, 4096, 4224

TM1 = 512
TM3 = 256
TQ = 1024
TK = 256
SB_SLABS = 4
SB_KV_PER_BODY = 2
assert TQ % TK == 0 and (TQ // TK) % SB_KV_PER_BODY == 0
FF_CH = 256
PAGES_PER_STEP = 32
VMEM_LIMIT = 56 * 1024 * 1024


def _nt(a, b):
    return lax.dot_general(a, b, (((1,), (1,)), ((), ())), preferred_element_type=F32)


def _tn(a, b):
    return lax.dot_general(a, b, (((0,), (0,)), ((), ())), preferred_element_type=F32)


def _nn(a, b):
    return jnp.dot(a, b, preferred_element_type=F32)


def _ln(x, g, b):
    mu = jnp.mean(x, axis=-1, keepdims=True)
    xc = x - mu
    var = jnp.mean(xc * xc, axis=-1, keepdims=True)
    return xc * lax.rsqrt(var + LN_EPS) * g + b


def _softplus(z):
    return jnp.maximum(z, 0.0) + jnp.log1p(jnp.exp(-jnp.abs(z)))


def _log_sigmoid(z):
    return jnp.minimum(z, 0.0) - jnp.log1p(jnp.exp(-jnp.abs(z)))


def _split_bf16(x):
    hi = x.astype(BF16)
    lo = (x - hi.astype(F32)).astype(BF16)
    return hi, lo


def _const_spec(shape):
    nd = len(shape)
    return pl.BlockSpec(shape, lambda *_: (0,) * nd, pipeline_mode=pl.Buffered(1))


def _layer_spec(arr, layer):
    nd = arr.ndim - 1
    return pl.BlockSpec((None,) + arr.shape[1:], lambda *_: (layer,) + (0,) * nd, pipeline_mode=pl.Buffered(1))


def _full_spec(shape):
    nd = len(shape)
    return pl.BlockSpec(shape, lambda *_: (0,) * nd)


def _proj_cols(xb, win_ref, bin_ref, wgate_ref, bgate_ref, c0, c1):
    if c0 >= C_GATE:
        w_ref, b_ref, c0, c1 = wgate_ref, bgate_ref, c0 - C_GATE, c1 - C_GATE
    else:
        w_ref, b_ref = win_ref, bin_ref
    return _nn(xb, w_ref[:, c0:c1]) + b_ref[:, c0:c1]


def _p1_kernel(x_ref, win_ref, bin_ref, wgate_ref, bgate_ref, ws_ref, bsf_ref, gmg_ref, gmb_ref, wa2_ref, ba_ref,
               gng_ref, gnb_ref, wbra_ref, wbrc_ref, tri_ref,
               kf_ref, vf_ref, qb_ref, kb_ref, vb_ref, mac_ref, gb_ref, st_ref, s2_scr):
    tm = x_ref.shape[1]

    @pl.when(pl.program_id(1) == 0)
    def _():
        s2_scr[...] = jnp.zeros_like(s2_scr)

    xb = x_ref[0].astype(BF16)
    proj = functools.partial(_proj_cols, xb, win_ref, bin_ref, wgate_ref, bgate_ref)

    q = proj(C_Q, C_K)
    k = proj(C_K, C_VV)
    v = proj(C_VV, C_GQ)
    kf_ref[0] = k
    vf_ref[0] = v
    qb_ref[0] = (q * (SB_HD ** -0.5)).astype(BF16)
    kb_ref[0] = k.astype(BF16)
    vb_ref[0] = v.astype(BF16)

    u = jax.nn.gelu(proj(C_U, C_V))
    vn = _ln(jax.nn.gelu(proj(C_V, C_Q)), gmg_ref[...], gmb_ref[...])
    vnb = vn.astype(BF16)
    r_i = lax.broadcasted_iota(jnp.int32, (GM_CHUNK, GM_CHUNK), 0)
    c_i = lax.broadcasted_iota(jnp.int32, (GM_CHUNK, GM_CHUNK), 1)
    ws_m = [jnp.where(r_i >= c_i, ws_ref[g], 0.0).astype(BF16) for g in range(GM_GROUPS)]
    sv_rows = []
    for n in range(tm // GM_CHUNK):
        cols = [_nn(ws_m[g], vnb[n * GM_CHUNK:(n + 1) * GM_CHUNK, g * LANES:(g + 1) * LANES])
                for g in range(GM_GROUPS)]
        sv_rows.append(jnp.concatenate(cols, axis=1) + bsf_ref[...])
    y_a = u * jnp.concatenate(sv_rows, axis=0)

    gq = proj(C_GQ, C_GK) * (GLA_DK ** -0.5)
    gk = proj(C_GK, C_GV)
    gv = proj(C_GV, C_GR).astype(BF16)
    ga = proj(C_GA, C_GATE).astype(BF16)
    la = _log_sigmoid(_nn(ga, wa2_ref[...]) + ba_ref[...]) * (1.0 / GLA_TAU)
    la_hi, la_lo = _split_bf16(la)
    tri = tri_ref[...]
    bcum = _nn(tri, la_hi) + _nn(tri, la_lo)
    head_of_lane = lax.broadcasted_iota(jnp.int32, (GLA_CHUNK, GLA_KW), 1) // GLA_DK
    t_i = lax.broadcasted_iota(jnp.int32, (GLA_CHUNK, GLA_CHUNK), 0)
    s_i = lax.broadcasted_iota(jnp.int32, (GLA_CHUNK, GLA_CHUNK), 1)
    causal = s_i <= t_i
    n_chunks = tm // GLA_CHUNK
    heads = range(GLA_HEADS)
    masks = [head_of_lane == h for h in heads]
    qe_m, v_h, decay, att, upd = [], [], [], [], []
    for c in range(n_chunks):
        r0 = c * GLA_CHUNK
        bc = bcum[r0:r0 + GLA_CHUNK]
        bref = bc[GLA_CHUNK // 2:GLA_CHUNK // 2 + 1]
        blast = bc[GLA_CHUNK - 1:GLA_CHUNK]
        qc = gq[r0:r0 + GLA_CHUNK]
        kc = gk[r0:r0 + GLA_CHUNK]
        vc = gv[r0:r0 + GLA_CHUNK]
        qe = qc * jnp.exp(bc)
        qt = qc * jnp.exp(bc - bref)
        kt = (kc * jnp.exp(bref - bc)).astype(BF16)
        kd = kc * jnp.exp(blast - bc)
        qe_m.append([jnp.where(masks[h], qe, 0.0).astype(BF16) for h in heads])
        v_h.append([vc[:, h * GLA_DV:(h + 1) * GLA_DV] for h in heads])
        decay.append(jnp.exp(blast))
        att.append([_nt(jnp.where(masks[h], qt, 0.0).astype(BF16), kt) for h in heads])
        u_c = _tn(v_h[c][0], jnp.where(masks[0], kd, 0.0).astype(BF16))
        for h in range(1, GLA_HEADS):
            u_c = u_c + _tn(v_h[c][h], jnp.where(masks[h], kd, 0.0).astype(BF16))
        upd.append(u_c)
    o_intra = [[_nn(jnp.where(causal, att[c][h], 0.0).astype(BF16), v_h[c][h]) for h in heads]
               for c in range(n_chunks)]
    state = s2_scr[...]
    states = []
    for c in range(n_chunks):
        states.append(state.astype(BF16))
        state = state * decay[c] + upd[c]
    s2_scr[...] = state
    st_ref[0] = state
    o_c = jnp.concatenate(
        [jnp.concatenate([o_intra[c][h] + _nt(qe_m[c][h], states[c]) for h in heads], axis=1)
         for c in range(n_chunks)], axis=0)
    o_n = jnp.concatenate(
        [_ln(o_c[:, h * GLA_DV:(h + 1) * GLA_DV], gng_ref[:, h * GLA_DV:(h + 1) * GLA_DV],
             gnb_ref[:, h * GLA_DV:(h + 1) * GLA_DV]) for h in range(GLA_HEADS)], axis=1)
    y_c = jax.nn.silu(proj(C_GR, C_GA)) * o_n

    g_a = jax.nn.sigmoid(proj(C_GATE, C_GATE + D_MODEL))
    g_c = jax.nn.sigmoid(proj(C_GATE + 2 * D_MODEL, C_GATE + 3 * D_MODEL))
    mac_ref[0] = g_a * _nn(y_a.astype(BF16), wbra_ref[...]) + g_c * _nn(y_c.astype(BF16), wbrc_ref[...])
    gb_ref[0] = jax.nn.sigmoid(proj(C_GATE + D_MODEL, C_GATE + 2 * D_MODEL))


def _p1_call(layer, x, params, tri):
    bsz, seq, _ = x.shape
    tm = TM1
    tile = lambda w: pl.BlockSpec((1, tm, w), lambda b, s: (b, s, 0))
    out_shape = (
        jax.ShapeDtypeStruct((bsz, seq, MIX_W), F32), jax.ShapeDtypeStruct((bsz, seq, MIX_W), F32),
        jax.ShapeDtypeStruct((bsz, seq, MIX_W), BF16), jax.ShapeDtypeStruct((bsz, seq, MIX_W), BF16),
        jax.ShapeDtypeStruct((bsz, seq, MIX_W), BF16),
        jax.ShapeDtypeStruct((bsz, seq, D_MODEL), F32), jax.ShapeDtypeStruct((bsz, seq, D_MODEL), F32),
        jax.ShapeDtypeStruct((bsz, GLA_DV, GLA_KW), F32),
    )
    out_specs = (tile(MIX_W), tile(MIX_W), tile(MIX_W), tile(MIX_W), tile(MIX_W), tile(D_MODEL), tile(D_MODEL),
                 pl.BlockSpec((1, GLA_DV, GLA_KW), lambda b, s: (b, 0, 0)))
    return pl.pallas_call(
        _p1_kernel,
        grid=(bsz, seq // tm),
        in_specs=[tile(D_MODEL)] + [_layer_spec(p, layer) for p in params] + [_const_spec(tri.shape)],
        out_specs=out_specs,
        out_shape=out_shape,
        scratch_shapes=[pltpu.VMEM((GLA_DV, GLA_KW), F32)],
        compiler_params=pltpu.CompilerParams(dimension_semantics=("arbitrary", "arbitrary"),
                                             vmem_limit_bytes=VMEM_LIMIT),
        name="prompt_proj_mixers",
    )(x, *params, tri)


def _p2_kernel(bias_ref, q_ref, k_ref, v_ref, u2_ref, o_ref):
    tq = q_ref.shape[1]
    tk = TK
    rs = tq // SB_SLABS
    n_diag = tq // tk
    hp = pl.program_id(1)
    i = pl.program_id(2)
    lane = lax.broadcasted_iota(jnp.int32, (rs, LANES), 1)
    t_i = lax.broadcasted_iota(jnp.int32, (rs, tk), 0)
    s_i = lax.broadcasted_iota(jnp.int32, (rs, tk), 1)
    u2 = u2_ref[...]
    sign_bit = jnp.uint32(0x80000000)
    chains = [(hh, sl) for sl in range(SB_SLABS) for hh in range(2)]
    q_of = []
    for hh, sl in chains:
        q_rows = q_ref[0, sl * rs:(sl + 1) * rs, :]
        q_of.append(jnp.where((lane < SB_HD) if hh == 0 else (lane >= SB_HD), q_rows, jnp.zeros_like(q_rows)))
    biases = (bias_ref[2 * hp], bias_ref[2 * hp + 1])

    def blocks(js, carry, key_offs):
        kblk = [k_ref[0, pl.ds(pl.multiple_of(j * tk, tk), tk), :] for j in js]
        vblk = [v_ref[0, pl.ds(pl.multiple_of(j * tk, tk), tk), :] for j in js]
        masks, live = {}, []
        for b, key_off in enumerate(key_offs):
            for idx, (hh, sl) in enumerate(chains):
                if key_off is not None and key_off + 1 >= (sl + 1) * rs:
                    continue
                live.append((b, idx))
                if key_off is not None and key_off + tk > sl * rs:
                    masks[b, idx] = s_i + key_off < t_i + sl * rs
        zs = {e: _nt(q_of[e[1]], kblk[e[0]]) for e in live}
        log_w0s, log_keeps, splits = {}, {}, {}
        for e in live:
            z = zs[e] + biases[chains[e[1]][0]]
            neg_abs = lax.bitcast_convert_type(lax.bitcast_convert_type(z, jnp.uint32) | sign_bit, F32)
            s = jnp.log(1.0 + jnp.exp(neg_abs))
            log_w0 = jnp.minimum(z, 0.0) - s
            log_keep = log_w0 - z
            if e in masks:
                log_keep = jnp.where(masks[e], log_keep, 0.0)
            log_w0s[e], log_keeps[e] = log_w0, log_keep
            splits[e] = log_keep.astype(BF16)
        sums = {e: _nn(splits[e], u2) for e in live}
        c_run = [carry[idx][0] for idx in range(len(chains))]
        ws = {}
        for e in live:
            b, idx = e
            w = jnp.exp(log_w0s[e] + (sums[e] + c_run[idx]))
            if e in masks:
                w = jnp.where(masks[e], w, 0.0)
            ws[e] = w.astype(BF16)
            c_run[idx] = c_run[idx] + (sums[e][:, 0:1] + log_keeps[e][:, 0:1])
        acc = [carry[idx][1] for idx in range(len(chains))]
        for e in live:
            acc[e[1]] = acc[e[1]] + _nn(ws[e], vblk[e[0]])
        return tuple((c_run[idx], acc[idx]) for idx in range(len(chains)))

    carry = tuple((jnp.zeros((rs, 1), F32), jnp.zeros((rs, LANES), F32)) for _ in chains)
    carry = blocks([i * n_diag + (n_diag - 1 - d) for d in range(n_diag)], carry,
                   [(n_diag - 1 - d) * tk for d in range(n_diag)])
    n_groups = (i * n_diag) // SB_KV_PER_BODY

    def body(it, cr):
        top = (n_groups - it) * SB_KV_PER_BODY - 1
        return blocks([top - b for b in range(SB_KV_PER_BODY)], cr, [None] * SB_KV_PER_BODY)

    carry = lax.fori_loop(0, n_groups, body, carry)
    for sl in range(SB_SLABS):
        o_ref[0, sl * rs:(sl + 1) * rs, :] = jnp.where(lane < SB_HD, carry[2 * sl][1],
                                                       carry[2 * sl + 1][1]).astype(o_ref.dtype)


def _p2_call(bias, qb, kb, vb, u):
    bsz, seq, _ = qb.shape
    n_hp = MIX_W // LANES
    return pl.pallas_call(
        _p2_kernel,
        grid=(bsz, n_hp, seq // TQ),
        in_specs=[pl.BlockSpec(memory_space=pltpu.SMEM),
                  pl.BlockSpec((1, TQ, LANES), lambda b, h, i: (b, i, h)),
                  pl.BlockSpec((1, seq, LANES), lambda b, h, i: (b, 0, h)),
                  pl.BlockSpec((1, seq, LANES), lambda b, h, i: (b, 0, h)),
                  pl.BlockSpec(u.shape, lambda b, h, i: (0, 0))],
        out_specs=pl.BlockSpec((1, TQ, LANES), lambda b, h, i: (b, i, h)),
        out_shape=jax.ShapeDtypeStruct((bsz, seq, MIX_W), BF16),
        compiler_params=pltpu.CompilerParams(dimension_semantics=("arbitrary", "arbitrary", "arbitrary"),
                                             vmem_limit_bytes=VMEM_LIMIT),
        name="prompt_stick_breaking",
    )(bias, qb, kb, vb, u)


def _channel_mixer_tail(x1, f, pb, wple_ref, wpg_ref, ln2g_ref, ln2b_ref):
    x1b = x1.astype(BF16)
    e = _nn(pb, wple_ref[...]) * jax.nn.sigmoid(_nn(x1b, wpg_ref[...]))
    return _ln(DN_ALPHA * x1 + f + e, ln2g_ref[...], ln2b_ref[...])


def _p3_kernel(x_ref, mac_ref, gb_ref, yb_ref, p_ref, wbrb_ref, wout_ref, wup_ref, wdown_ref, wple_ref, wpg_ref,
               ln1g_ref, ln1b_ref, cw_ref, cb_ref, ln2g_ref, ln2b_ref,
               x2_ref, conv_ref, hs_scr):
    tm = x_ref.shape[1]
    pad = SUBLANES

    @pl.when(pl.program_id(1) == 0)
    def _():
        hs_scr[0:pad, :] = jnp.zeros((pad, D2), F32)

    merged = mac_ref[0] + gb_ref[0] * _nn(yb_ref[0], wbrb_ref[...])
    h = _nn(merged.astype(BF16), wout_ref[...])
    x1 = _ln(DN_ALPHA * x_ref[0] + h, ln1g_ref[...], ln1b_ref[...])
    x1b = x1.astype(BF16)

    def up(c0):
        return [_nn(x1b, wup_ref[:, base:base + FF_CH]) for base in (c0, D_FF + c0)]

    def conv_act(c0, hups):
        halves = []
        for base, hup in zip((c0, D_FF + c0), hups):
            cols = slice(base, base + FF_CH)
            hs_scr[pad:pad + tm, cols] = hup
            hc = cb_ref[:, cols] + cw_ref[CONV_W - 1:CONV_W, cols] * hup
            for i in range(CONV_W - 1):
                hc = hc + cw_ref[i:i + 1, cols] * hs_scr[pad - 2 + i:pad - 2 + i + tm, cols]
            halves.append(hc)
        return (jax.nn.gelu(halves[0]) * halves[1]).astype(BF16)

    chunks = list(range(0, D_FF, FF_CH))
    f = jnp.zeros((tm, D_MODEL), F32)
    hups = up(chunks[0])
    act_prev = None
    for n, c0 in enumerate(chunks):
        if act_prev is not None:
            f = f + _nn(act_prev, wdown_ref[chunks[n - 1]:chunks[n - 1] + FF_CH, :])
        hups_next = up(chunks[n + 1]) if n + 1 < len(chunks) else None
        act_prev = conv_act(c0, hups)
        hups = hups_next
    f = f + _nn(act_prev, wdown_ref[chunks[-1]:chunks[-1] + FF_CH, :])
    last = hs_scr[pad + tm - 2:pad + tm, :]
    hs_scr[pad - 2:pad, :] = last
    conv_ref[0] = last

    x2_ref[0] = _channel_mixer_tail(x1, f, p_ref[0].astype(BF16), wple_ref, wpg_ref, ln2g_ref, ln2b_ref)


def _p3_call(layer, x, mac, gb, yb, p_all, params):
    bsz, seq, _ = x.shape
    tm = TM3
    tile = lambda w: pl.BlockSpec((1, tm, w), lambda b, s: (b, s, 0))
    return pl.pallas_call(
        _p3_kernel,
        grid=(bsz, seq // tm),
        in_specs=[tile(D_MODEL), tile(D_MODEL), tile(D_MODEL), tile(MIX_W),
                  pl.BlockSpec((None, 1, tm, PLE_DIM), lambda b, s: (layer, b, s, 0))]
                 + [_layer_spec(p, layer) for p in params],
        out_specs=(tile(D_MODEL), pl.BlockSpec((1, CONV_W - 1, D2), lambda b, s: (b, 0, 0))),
        out_shape=(jax.ShapeDtypeStruct((bsz, seq, D_MODEL), F32),
                   jax.ShapeDtypeStruct((bsz, CONV_W - 1, D2), F32)),
        scratch_shapes=[pltpu.VMEM((tm + SUBLANES, D2), F32)],
        compiler_params=pltpu.CompilerParams(dimension_semantics=("arbitrary", "arbitrary"),
                                             vmem_limit_bytes=VMEM_LIMIT),
        name="prompt_channel_mixer",
    )(x, mac, gb, yb, p_all, *params)


def _d1_kernel(x_ref, win_ref, bin_ref, wgate_ref, bgate_ref, ws0_ref, bs0_ref, gmg_ref, gmb_ref, wa2_ref, ba_ref,
               gng_ref, gnb_ref, wbra_ref, wbrc_ref, st_ref,
               kf_ref, vf_ref, qf_ref, gmv_ref, mac_ref, gb_ref, stn_ref, oc_scr):
    nb = x_ref.shape[0]
    xb = x_ref[...].astype(BF16)
    proj = functools.partial(_proj_cols, xb, win_ref, bin_ref, wgate_ref, bgate_ref)

    qf_ref[...] = proj(C_Q, C_K) * (SB_HD ** -0.5)
    kf_ref[...] = proj(C_K, C_VV)
    vf_ref[...] = proj(C_VV, C_GQ)

    u = jax.nn.gelu(proj(C_U, C_V))
    vn = _ln(jax.nn.gelu(proj(C_V, C_Q)), gmg_ref[...], gmb_ref[...])
    gmv_ref[...] = vn
    y_a = u * (ws0_ref[...] * vn + bs0_ref[...])

    gq = proj(C_GQ, C_GK) * (GLA_DK ** -0.5)
    gk = proj(C_GK, C_GV)
    gv = proj(C_GV, C_GR)
    ga = proj(C_GA, C_GATE).astype(BF16)
    a = jnp.exp(_log_sigmoid(_nn(ga, wa2_ref[...]) + ba_ref[...]) * (1.0 / GLA_TAU))

    def cols(m):
        return jnp.concatenate([m, jnp.zeros((LANES - nb, GLA_KW), F32)], axis=0).T

    a_t, k_t, q_t = cols(a), cols(gk), cols(gq)
    for b in range(nb):
        s_old = st_ref[b]
        v_exp = jnp.concatenate(
            [jnp.broadcast_to(gv[b:b + 1, h * GLA_DV:(h + 1) * GLA_DV], (GLA_DK, GLA_DV)) for h in range(GLA_HEADS)],
            axis=0)
        s_new = a_t[:, b:b + 1] * s_old + k_t[:, b:b + 1] * v_exp
        stn_ref[b] = s_new
        oq = q_t[:, b:b + 1] * s_new
        for h in range(GLA_HEADS):
            oc_scr[b:b + 1, h * GLA_DV:(h + 1) * GLA_DV] = jnp.sum(oq[h * GLA_DK:(h + 1) * GLA_DK], axis=0,
                                                                     keepdims=True)
    o_c = oc_scr[...]
    o_n = jnp.concatenate(
        [_ln(o_c[:, h * GLA_DV:(h + 1) * GLA_DV], gng_ref[:, h * GLA_DV:(h + 1) * GLA_DV],
             gnb_ref[:, h * GLA_DV:(h + 1) * GLA_DV]) for h in range(GLA_HEADS)], axis=1)
    y_c = jax.nn.silu(proj(C_GR, C_GA)) * o_n

    g_a = jax.nn.sigmoid(proj(C_GATE, C_GATE + D_MODEL))
    g_c = jax.nn.sigmoid(proj(C_GATE + 2 * D_MODEL, C_GATE + 3 * D_MODEL))
    mac_ref[...] = g_a * _nn(y_a.astype(BF16), wbra_ref[...]) + g_c * _nn(y_c.astype(BF16), wbrc_ref[...])
    gb_ref[...] = jax.nn.sigmoid(proj(C_GATE + D_MODEL, C_GATE + 2 * D_MODEL))


def _d1_call(layer, x, params, st_all):
    nb = x.shape[0]
    out_shape = (
        jax.ShapeDtypeStruct((nb, MIX_W), F32), jax.ShapeDtypeStruct((nb, MIX_W), F32),
        jax.ShapeDtypeStruct((nb, MIX_W), F32), jax.ShapeDtypeStruct((nb, MIX_W), F32),
        jax.ShapeDtypeStruct((nb, D_MODEL), F32), jax.ShapeDtypeStruct((nb, D_MODEL), F32),
        jax.ShapeDtypeStruct((nb, GLA_KW, GLA_DV), F32),
    )
    return pl.pallas_call(
        _d1_kernel,
        grid=(1,),
        in_specs=[_full_spec(x.shape)] + [_layer_spec(p, layer) for p in params] + [_layer_spec(st_all, layer)],
        out_specs=tuple(_full_spec(o.shape) for o in out_shape),
        out_shape=out_shape,
        scratch_shapes=[pltpu.VMEM((nb, MIX_W), F32)],
        compiler_params=pltpu.CompilerParams(dimension_semantics=("arbitrary",), vmem_limit_bytes=VMEM_LIMIT),
        name="sample_proj_mixers",
    )(x, *params, st_all)


def _d2_kernel(pt_ref, q_ref, bias_ref, uo_ref, glater_ref, gall_ref, *refs):
    n = PAGES_PER_STEP
    k_refs, v_refs = refs[:n], refs[n:2 * n]
    o_ref, acc_scr, c_scr = refs[2 * n], refs[2 * n + 1], refs[2 * n + 2]
    j = pl.program_id(1)

    @pl.when(j == 0)
    def _():
        acc_scr[...] = jnp.zeros_like(acc_scr)
        c_scr[...] = jnp.zeros_like(c_scr)

    row_head = lax.broadcasted_iota(jnp.int32, (SB_HEADS, MIX_W), 0)
    col_head = lax.broadcasted_iota(jnp.int32, (SB_HEADS, MIX_W), 1) // SB_HD
    qbd = jnp.where(row_head == col_head, jnp.broadcast_to(q_ref[0], (SB_HEADS, MIX_W)), 0.0).astype(BF16)
    z = jnp.concatenate([_nn(qbd, k_refs[i][0].astype(BF16)) for i in range(n)], axis=0) + bias_ref[...]
    sp = _softplus(z)
    log_keep = -sp
    hi, lo = _split_bf16(log_keep)
    r2 = _nn(hi, uo_ref[...]) + _nn(lo, uo_ref[...])
    r, tot = r2[:, :PAGE], r2[:, PAGE:]
    t_hi, t_lo = _split_bf16(tot)
    later = _nn(glater_ref[...], t_hi) + _nn(glater_ref[...], t_lo)
    c = c_scr[...]
    w = jnp.exp((z - sp) + (r + (c + later)))
    c_scr[...] = c + _nn(gall_ref[...], t_hi) + _nn(gall_ref[...], t_lo)
    for h in range(SB_HEADS):
        rows = slice(h * SB_HD, (h + 1) * SB_HD)
        acc_h = acc_scr[rows, :]
        for i in range(n):
            acc_h = acc_h + v_refs[i][0, rows, :] * w[i * SB_HEADS + h:i * SB_HEADS + h + 1, :]
        acc_scr[rows, :] = acc_h

    @pl.when(j == pl.num_programs(1) - 1)
    def _():
        o_ref[0] = jnp.sum(acc_scr[...], axis=1, keepdims=True)


def _d2_call(layer, page_table, q, bias_rows, uo, glater, gall, cache_k_t, cache_v_t):
    nb, n_pages = page_table.shape
    n = PAGES_PER_STEP
    steps = n_pages // n

    def page_spec(i):
        return pl.BlockSpec((None, 1, MIX_W, PAGE),
                            lambda b, j, pt: (layer, pt[b, n_pages - 1 - (j * n + i)], 0, 0))

    const = lambda shape: pl.BlockSpec(shape, lambda b, j, pt: (0,) * len(shape))
    grid_spec = pltpu.PrefetchScalarGridSpec(
        num_scalar_prefetch=1,
        grid=(nb, steps),
        in_specs=[pl.BlockSpec((1, 1, MIX_W), lambda b, j, pt: (b, 0, 0)),
                  const(bias_rows.shape), const(uo.shape), const(glater.shape), const(gall.shape)]
                 + [page_spec(i) for i in range(n)] + [page_spec(i) for i in range(n)],
        out_specs=pl.BlockSpec((1, MIX_W, 1), lambda b, j, pt: (b, 0, 0)),
        scratch_shapes=[pltpu.VMEM((MIX_W, PAGE), F32), pltpu.VMEM((n * SB_HEADS, PAGE), F32)],
    )
    return pl.pallas_call(
        _d2_kernel,
        grid_spec=grid_spec,
        out_shape=jax.ShapeDtypeStruct((nb, MIX_W, 1), F32),
        compiler_params=pltpu.CompilerParams(dimension_semantics=("arbitrary", "arbitrary"),
                                             vmem_limit_bytes=VMEM_LIMIT),
        name="sample_paged_stick_breaking",
    )(page_table, q, bias_rows, uo, glater, gall, *([cache_k_t] * n), *([cache_v_t] * n))


def _d3_kernel(x_ref, mac_ref, gb_ref, yb_ref, p_ref, prev0_ref, prev1_ref, wbrb_ref, wout_ref, wup_ref, wdown_ref,
               wple_ref, wpg_ref, ln1g_ref, ln1b_ref, cw_ref, cb_ref, ln2g_ref, ln2b_ref,
               x2_ref, hup_ref):
    nb = x_ref.shape[0]
    merged = mac_ref[...] + gb_ref[...] * _nn(yb_ref[...].astype(BF16), wbrb_ref[...])
    h = _nn(merged.astype(BF16), wout_ref[...])
    x1 = _ln(DN_ALPHA * x_ref[...] + h, ln1g_ref[...], ln1b_ref[...])
    x1b = x1.astype(BF16)
    f = jnp.zeros((nb, D_MODEL), F32)
    for c0 in range(0, D_FF, FF_CH):
        halves = []
        for base in (c0, D_FF + c0):
            sl = slice(base, base + FF_CH)
            hup = _nn(x1b, wup_ref[:, sl])
            hup_ref[:, sl] = hup
            halves.append(cb_ref[:, sl] + cw_ref[0:1, sl] * prev0_ref[:, sl] + cw_ref[1:2, sl] * prev1_ref[:, sl]
                          + cw_ref[2:3, sl] * hup)
        act = (jax.nn.gelu(halves[0]) * halves[1]).astype(BF16)
        f = f + _nn(act, wdown_ref[c0:c0 + FF_CH, :])
    x2_ref[...] = _channel_mixer_tail(x1, f, p_ref[...].astype(BF16), wple_ref, wpg_ref, ln2g_ref, ln2b_ref)


def _d3_call(layer, x, mac, gb, yb, p_all, prev0, prev1, params):
    nb = x.shape[0]
    acts = (x, mac, gb, yb)
    out_shape = (jax.ShapeDtypeStruct((nb, D_MODEL), F32), jax.ShapeDtypeStruct((nb, D2), F32))
    return pl.pallas_call(
        _d3_kernel,
        grid=(1,),
        in_specs=[_full_spec(a.shape) for a in acts] + [_layer_spec(p_all, layer)]
                 + [_full_spec(prev0.shape), _full_spec(prev1.shape)] + [_layer_spec(p, layer) for p in params],
        out_specs=tuple(_full_spec(o.shape) for o in out_shape),
        out_shape=out_shape,
        compiler_params=pltpu.CompilerParams(dimension_semantics=("arbitrary",), vmem_limit_bytes=VMEM_LIMIT),
        name="sample_channel_mixer",
    )(*acts, p_all, prev0, prev1, *params)


def _constants():
    tri = np.zeros((TM1, TM1), np.float32)
    for c in range(TM1 // GLA_CHUNK):
        r0 = c * GLA_CHUNK
        tri[r0:r0 + GLA_CHUNK, r0:r0 + GLA_CHUNK] = np.tril(np.ones((GLA_CHUNK, GLA_CHUNK), np.float32))
    jj = np.arange(TK)
    u_sb = (jj[:, None] > jj[None, :]).astype(np.float32)
    kk = np.arange(PAGE)
    uo = np.concatenate([(kk[:, None] > kk[None, :]).astype(np.float32), np.ones((PAGE, PAGE), np.float32)], axis=1)
    rows = np.arange(PAGES_PER_STEP * SB_HEADS)
    page_of, head_of = rows // SB_HEADS, rows % SB_HEADS
    same_head = head_of[:, None] == head_of[None, :]
    glater = (same_head & (page_of[None, :] < page_of[:, None])).astype(np.float32)
    gall = same_head.astype(np.float32)
    as_bf16 = lambda m: jnp.asarray(m, BF16)
    return as_bf16(tri), as_bf16(u_sb), as_bf16(uo), as_bf16(glater), as_bf16(gall)


def _row(vec):
    return vec.reshape(1, -1)


def kernel(x_prompt, x_sample, cache_k, cache_v, state_gla, state_conv, page_table, p_prompt, p_sample, w_in, b_in, gm_ws, gm_bs, gm_ln_g, gm_ln_b, sb_bias, gla_wa2, gla_ba, gla_gn_g, gla_gn_b, w_br_a, w_br_b, w_br_c, w_out, ln1_g, ln1_b, w_up, conv_w, conv_b, w_down, w_ple, w_pgate, ln2_g, ln2_b):
    bp, sp = x_prompt.shape[0], x_prompt.shape[1]
    bd = x_sample.shape[0]
    n_pool = cache_k.shape[1]
    tri, u_sb, uo, glater, gall = _constants()
    ck = jnp.transpose(cache_k, (0, 1, 3, 4, 2)).reshape(DEPTH, n_pool, MIX_W, PAGE)
    cv = jnp.transpose(cache_v, (0, 1, 3, 4, 2)).reshape(DEPTH, n_pool, MIX_W, PAGE)

    split = C_GA + GLA_RANK
    ga_pad = LANES - GLA_RANK
    win = jnp.pad(w_in[:, :, :split], ((0, 0), (0, 0), (0, ga_pad))).astype(BF16)
    bin_ = jnp.pad(b_in[:, :split], ((0, 0), (0, ga_pad))).reshape(DEPTH, 1, C_GATE)
    wgate = w_in[:, :, split:].astype(BF16)
    bgate = b_in[:, split:].reshape(DEPTH, 1, 3 * D_MODEL)
    wa2 = jnp.concatenate([gla_wa2, jnp.zeros((DEPTH, LANES - GLA_RANK, GLA_KW), F32)], axis=1).astype(BF16)
    bsf = jnp.repeat(jnp.swapaxes(gm_bs, 1, 2), LANES, axis=2)
    ws0 = jnp.repeat(gm_ws[:, :, 0, 0], LANES, axis=1).reshape(DEPTH, 1, MIX_W)
    bs0 = jnp.repeat(gm_bs[:, :, 0], LANES, axis=1).reshape(DEPTH, 1, MIX_W)
    rows = lambda v: v.reshape(DEPTH, 1, -1)
    mixer_tail = (rows(gm_ln_g), rows(gm_ln_b), wa2, rows(gla_ba), rows(gla_gn_g), rows(gla_gn_b),
                  w_br_a.astype(BF16), w_br_c.astype(BF16))
    p1_params = (win, bin_, wgate, bgate, gm_ws, bsf) + mixer_tail
    d1_params = (win, bin_, wgate, bgate, ws0, bs0) + mixer_tail
    p3_params = (w_br_b.astype(BF16), w_out.astype(BF16), w_up.astype(BF16), w_down.astype(BF16),
                 w_ple.astype(BF16), w_pgate.astype(BF16), rows(ln1_g), rows(ln1_b), conv_w, rows(conv_b),
                 rows(ln2_g), rows(ln2_b))
    st_all = state_gla.reshape(DEPTH, bd, GLA_KW, GLA_DV)
    p_sample_all = p_sample.reshape(DEPTH, bd, PLE_DIM)

    yp = x_prompt
    yd = x_sample.reshape(bd, D_MODEL)
    outs = [[] for _ in range(9)]
    for i in range(DEPTH):
        kf, vf, qb, kb, vb, mac, gb, s2 = _p1_call(i, yp, p1_params, tri)
        yb = _p2_call(sb_bias[i], qb, kb, vb, u_sb)
        yp, conv_p = _p3_call(i, yp, mac, gb, yb, p_prompt, p3_params)
        gla_p = s2.reshape(bp, GLA_DV, GLA_HEADS, GLA_DK).transpose(0, 2, 3, 1)

        kd, vd, qd, gmv, mac_d, gb_d, st_new = _d1_call(i, yd, d1_params, st_all)
        bias_rows = jnp.broadcast_to(jnp.tile(sb_bias[i], PAGES_PER_STEP)[:, None],
                                     (PAGES_PER_STEP * SB_HEADS, PAGE))
        yb_d = _d2_call(i, page_table, qd.reshape(bd, 1, MIX_W), bias_rows, uo, glater, gall, ck, cv)
        yb_d = yb_d.reshape(bd, MIX_W)
        yd, hup_d = _d3_call(i, yd, mac_d, gb_d, yb_d, p_sample_all, state_conv[i][:, 0], state_conv[i][:, 1],
                             p3_params)

        outs[0].append(kf.reshape(bp, sp, SB_HEADS, SB_HD))
        outs[1].append(vf.reshape(bp, sp, SB_HEADS, SB_HD))
        outs[2].append(kd.reshape(bd, 1, SB_HEADS, SB_HD))
        outs[3].append(vd.reshape(bd, 1, SB_HEADS, SB_HD))
        outs[4].append(gla_p)
        outs[5].append(st_new.reshape(bd, GLA_HEADS, GLA_DK, GLA_DV))
        outs[6].append(conv_p)
        outs[7].append(jnp.stack([state_conv[i][:, 1], hup_d], axis=1))
        outs[8].append(gmv.reshape(bd, 1, MIX_W))

    return (yp, yd.reshape(bd, 1, D_MODEL)) + tuple(jnp.stack(o) for o in outs)
```
